```python
import math
import jax, jax.numpy as jnp
from jax import lax
import numpy as np

D_MODEL = 2048
BATCH = 1
SEQ = 8192
DEPTH = 1
DEC_BATCH = 2
DEC_SEQ = 4096
PAST_LEN = 128

PLE_DIM = 256
ATTN_HEADS = 8
ATTN_QK_DIM = 64
ATTN_V_DIM = 128
ATTN_WIDTH = ATTN_HEADS * ATTN_V_DIM
REC_HEADS = 8
REC_K_DIM = 128
REC_V_DIM = 128
REC_WIDTH = REC_HEADS * REC_V_DIM
MIX_WIDTH = ATTN_WIDTH + REC_WIDTH
N_EXPERTS = 16
CAPACITY_FACTOR = 2
D_EXPERT = 2048
Q_BLOCK = 128
REC_CHUNK = 64
NORM_EPS = 1e-6
SPLIT_SIZES = (ATTN_HEADS * 2 * ATTN_QK_DIM, ATTN_HEADS * 2 * ATTN_QK_DIM, ATTN_WIDTH,
               REC_HEADS * REC_K_DIM, REC_HEADS * REC_K_DIM, REC_HEADS * REC_K_DIM,
               REC_WIDTH, REC_WIDTH)
IN_WIDTH = sum(SPLIT_SIZES)

kernel_name = "hymba_diffattn_hgrn2_ec_moe_encoder"


def rms_norm(x, g):
    xf = x.astype(jnp.float32)
    y = xf * lax.rsqrt(jnp.mean(xf * xf, axis=-1, keepdims=True) + NORM_EPS)
    return (y * g.astype(jnp.float32)).astype(x.dtype)


def split_columns(u):
    parts = []
    off = 0
    for sz in SPLIT_SIZES:
        parts.append(u[..., off:off + sz])
        off += sz
    return parts


def diff_attention(q, k, v, lam):
    B, T = q.shape[0], q.shape[1]
    n_blk = T // Q_BLOCK
    slopes = jnp.exp2(-8.0 * jnp.arange(1, ATTN_HEADS + 1, dtype=jnp.float32) / ATTN_HEADS)
    pos = jnp.arange(T, dtype=jnp.float32)
    vf = v.astype(jnp.float32)
    kf = k.astype(jnp.float32)
    q_blocks = q.astype(jnp.float32).reshape(B, n_blk, Q_BLOCK, ATTN_HEADS, 2, ATTN_QK_DIM).transpose(1, 0, 2, 3, 4, 5)
    q_pos = pos.reshape(n_blk, Q_BLOCK)

    def one_block(args):
        qb, qp = args
        s = jnp.einsum('bqhcd,bkhcd->bhcqk', qb, kf)
        dist = jnp.abs(qp[:, None] - pos[None, :])
        p = jax.nn.softmax(s - slopes[:, None, None, None] * dist, axis=-1)
        a = p[:, :, 0] - lam * p[:, :, 1]
        return jnp.einsum('bhqk,bkhd->bqhd', a, vf)

    out = lax.map(one_block, (q_blocks, q_pos))
    return out.transpose(1, 0, 2, 3, 4).reshape(B, T, ATTN_HEADS, ATTN_V_DIM)


def gated_recurrence_chunked(q, k, v, log_f):
    B, H, T, DK = q.shape
    DV = v.shape[-1]
    n = T // REC_CHUNK

    def chunks(a):
        return a.reshape(B, H, n, REC_CHUNK, a.shape[-1]).transpose(2, 0, 1, 3, 4)

    idx_c = jnp.arange(REC_CHUNK)
    causal = (idx_c[:, None] >= idx_c[None, :])[:, :, None]

    def step(S, inp):
        qc, kc, vc, gc = inp
        b = jnp.cumsum(gc, axis=-2)
        o_inter = jnp.einsum('bhtk,bhkv->bhtv', qc * jnp.exp(b), S)
        diff = b[:, :, :, None, :] - b[:, :, None, :, :]
        decay = jnp.where(causal, jnp.exp(jnp.where(causal, diff, 0.0)), 0.0)
        scores = jnp.einsum('bhtk,bhsk,bhtsk->bhts', qc, kc, decay)
        o_intra = jnp.einsum('bhts,bhsv->bhtv', scores, vc)
        b_end = b[:, :, -1:, :]
        S_new = jnp.exp(b_end)[:, :, 0, :, None] * S + jnp.einsum(
            'bhsk,bhsv->bhkv', kc * jnp.exp(b_end - b), vc)
        return S_new, o_inter + o_intra

    S0 = jnp.zeros((B, H, DK, DV), jnp.float32)
    _, o = lax.scan(step, S0, (chunks(q), chunks(k), chunks(v), chunks(log_f)))
    return o.transpose(1, 2, 0, 3, 4).reshape(B, H, T, DV)


def expert_choice_ffn(h, w_router, w_gate, w_up, w_down):
    N = h.shape[0]
    cap = max(1, (CAPACITY_FACTOR * N) // N_EXPERTS)
    aff = jax.nn.softmax((h @ w_router).astype(jnp.float32), axis=-1)
    gates, idx = lax.top_k(aff.T, cap)
    xe = h[idx]
    hid = jax.nn.silu(jnp.einsum('ecd,edf->ecf', xe, w_gate)) * jnp.einsum('ecd,edf->ecf', xe, w_up)
    ye = jnp.einsum('ecf,efd->ecd', hid, w_down) * gates[..., None].astype(hid.dtype)
    return jnp.zeros((N, ye.shape[-1]), ye.dtype).at[idx.reshape(-1)].add(ye.reshape(-1, ye.shape[-1]))


def encoder_layer(x, p_i, i, norm_mix_g, w_in, q_norm_g, k_norm_g, diff_lambda, diff_norm_g,
                  rec_lower_bound, rec_norm_g, w_out, norm_ffn_g, w_router, w_expert_gate,
                  w_expert_up, w_expert_down, norm_ple_g, w_ple_gate, w_ple_proj):
    B, T, D = x.shape
    h = rms_norm(x, norm_mix_g[i])
    u = h @ w_in[i]
    a_q, a_k, a_v, r_q, r_f_fwd, r_f_bwd, r_i, r_g = split_columns(u)

    q = rms_norm(a_q.reshape(B, T, ATTN_HEADS, 2, ATTN_QK_DIM), q_norm_g[i]) * (ATTN_QK_DIM ** -0.5)
    k = rms_norm(a_k.reshape(B, T, ATTN_HEADS, 2, ATTN_QK_DIM), k_norm_g[i])
    v = a_v.reshape(B, T, ATTN_HEADS, ATTN_V_DIM)
    lam_init = 0.8 - 0.6 * math.exp(-0.3 * i)
    lp = diff_lambda[i].astype(jnp.float32)
    lam = jnp.exp(jnp.sum(lp[0] * lp[1])) - jnp.exp(jnp.sum(lp[2] * lp[3])) + lam_init
    ao = diff_attention(q, k, v, lam)
    ao = rms_norm(ao, diff_norm_g[i].reshape(ATTN_HEADS, ATTN_V_DIM)) * (1.0 - lam_init)
    ao = ao.reshape(B, T, ATTN_WIDTH)

    lb_all = jnp.cumsum(jax.nn.softmax(rec_lower_bound.astype(jnp.float32), axis=1), axis=1)
    lb = lb_all[:, i]

    def heads(a, d):
        return a.astype(jnp.float32).reshape(B, T, REC_HEADS, d).transpose(0, 2, 1, 3)

    rq = heads(jax.nn.silu(r_q.astype(jnp.float32)), REC_K_DIM)
    rv = heads(r_i, REC_V_DIM)

    def gates(z, lb_d):
        z = z.astype(jnp.float32)
        log_f = jnp.log(lb_d + (1.0 - lb_d) * jax.nn.sigmoid(z))
        kk = (1.0 - lb_d) * jax.nn.sigmoid(-z)
        return heads(kk, REC_K_DIM), heads(log_f, REC_K_DIM)

    k_f, g_f = gates(r_f_fwd, lb[0])
    k_b, g_b = gates(r_f_bwd, lb[1])
    o_fwd = gated_recurrence_chunked(rq, k_f, rv, g_f)
    o_bwd = jnp.flip(gated_recurrence_chunked(jnp.flip(rq, 2), jnp.flip(k_b, 2),
                                              jnp.flip(rv, 2), jnp.flip(g_b, 2)), 2)
    ro = (o_fwd + o_bwd).transpose(0, 2, 1, 3)
    ro = rms_norm(ro, rec_norm_g[i].reshape(REC_HEADS, REC_V_DIM)).reshape(B, T, REC_WIDTH)
    ro = ro * jax.nn.silu(r_g.astype(jnp.float32))

    mixed = jnp.concatenate([ao.astype(jnp.float32), ro], axis=-1).astype(x.dtype)
    x = x + mixed @ w_out[i]

    h2 = rms_norm(x, norm_ffn_g[i]).reshape(B * T, D)
    x = x + expert_choice_ffn(h2, w_router[i], w_expert_gate[i], w_expert_up[i],
                              w_expert_down[i]).reshape(B, T, D).astype(x.dtype)

    gate = jax.nn.sigmoid((rms_norm(x, norm_ple_g[i]) @ w_ple_gate[i]).astype(jnp.float32))
    x = x + (gate * (p_i @ w_ple_proj[i]).astype(jnp.float32)).astype(x.dtype)
    return x


def run_trunk(x, p, norm_mix_g, w_in, q_norm_g, k_norm_g, diff_lambda, diff_norm_g,
              rec_lower_bound, rec_norm_g, w_out, norm_ffn_g, w_router, w_expert_gate,
              w_expert_up, w_expert_down, norm_ple_g, w_ple_gate, w_ple_proj):
    h = x
    for i in range(DEPTH):
        h = encoder_layer(h, p[i], i, norm_mix_g, w_in, q_norm_g, k_norm_g, diff_lambda, diff_norm_g,
                          rec_lower_bound, rec_norm_g, w_out, norm_ffn_g, w_router, w_expert_gate,
                          w_expert_up, w_expert_down, norm_ple_g, w_ple_gate, w_ple_proj)
    return h.astype(x.dtype)


def setup_inputs(seed: int = 0) -> dict:
    key = jax.random.key(seed)
    ks = jax.random.split(key, 21)
    f32 = jnp.float32

    def nrm(k, shape, scale):
        return jax.random.normal(k, shape, f32) * scale

    def gain(k, shape):
        return 1.0 + 0.02 * jax.random.normal(k, shape, f32)

    return {
        "x_prompt": nrm(ks[0], (BATCH, SEQ, D_MODEL), 1.0),
        "x_sample": nrm(ks[1], (DEC_BATCH, DEC_SEQ, D_MODEL), 1.0),
        "p_prompt": nrm(ks[2], (DEPTH, BATCH, SEQ, PLE_DIM), 1.0),
        "p_sample": nrm(ks[3], (DEPTH, DEC_BATCH, DEC_SEQ, PLE_DIM), 1.0),
        "norm_mix_g": gain(ks[4], (DEPTH, D_MODEL)),
        "w_in": nrm(ks[5], (DEPTH, D_MODEL, IN_WIDTH), D_MODEL ** -0.5),
        "q_norm_g": gain(ks[6], (DEPTH, ATTN_QK_DIM)),
        "k_norm_g": gain(ks[7], (DEPTH, ATTN_QK_DIM)),
        "diff_lambda": nrm(ks[8], (DEPTH, 4, ATTN_QK_DIM), 0.1),
        "diff_norm_g": gain(ks[9], (DEPTH, ATTN_WIDTH)),
        "rec_lower_bound": nrm(ks[10], (2, DEPTH + 1, REC_HEADS * REC_K_DIM), 0.1),
        "rec_norm_g": gain(ks[11], (DEPTH, REC_WIDTH)),
        "w_out": nrm(ks[12], (DEPTH, MIX_WIDTH, D_MODEL), MIX_WIDTH ** -0.5),
        "norm_ffn_g": gain(ks[13], (DEPTH, D_MODEL)),
        "w_router": nrm(ks[14], (DEPTH, D_MODEL, N_EXPERTS), D_MODEL ** -0.5),
        "w_expert_gate": nrm(ks[15], (DEPTH, N_EXPERTS, D_MODEL, D_EXPERT), D_MODEL ** -0.5),
        "w_expert_up": nrm(ks[16], (DEPTH, N_EXPERTS, D_MODEL, D_EXPERT), D_MODEL ** -0.5),
        "w_expert_down": nrm(ks[17], (DEPTH, N_EXPERTS, D_EXPERT, D_MODEL), D_EXPERT ** -0.5),
        "norm_ple_g": gain(ks[18], (DEPTH, D_MODEL)),
        "w_ple_gate": nrm(ks[19], (DEPTH, D_MODEL, D_MODEL), D_MODEL ** -0.5),
        "w_ple_proj": nrm(ks[20], (DEPTH, PLE_DIM, D_MODEL), PLE_DIM ** -0.5),
    }


def reference(x_prompt, x_sample, p_prompt, p_sample, norm_mix_g, w_in, q_norm_g, k_norm_g,
              diff_lambda, diff_norm_g, rec_lower_bound, rec_norm_g, w_out, norm_ffn_g, w_router,
              w_expert_gate, w_expert_up, w_expert_down, norm_ple_g, w_ple_gate, w_ple_proj):
    y_prompt = run_trunk(x_prompt, p_prompt, norm_mix_g, w_in, q_norm_g, k_norm_g, diff_lambda,
                         diff_norm_g, rec_lower_bound, rec_norm_g, w_out, norm_ffn_g, w_router,
                         w_expert_gate, w_expert_up, w_expert_down, norm_ple_g, w_ple_gate, w_ple_proj)
    y_sample = run_trunk(x_sample, p_sample, norm_mix_g, w_in, q_norm_g, k_norm_g, diff_lambda,
                         diff_norm_g, rec_lower_bound, rec_norm_g, w_out, norm_ffn_g, w_router,
                         w_expert_gate, w_expert_up, w_expert_down, norm_ple_g, w_ple_gate, w_ple_proj)
    return (y_prompt, y_sample)
```

```python
import functools
import math

import numpy as np
import jax
import jax.numpy as jnp
from jax import lax
from jax.experimental import pallas as pl
from jax.experimental.pallas import tpu as pltpu

F32 = jnp.float32
BF16 = jnp.bfloat16
I32 = jnp.int32

D_MODEL = 2048
PLE_DIM = 256
HEADS = 8
QK_DIM = 64
V_DIM = 128
HEAD_W = 128
ATTN_W = HEADS * V_DIM
REC_W = HEADS * V_DIM
N_EXPERTS = 16
CAPACITY_FACTOR = 2
D_EXPERT = 2048
NORM_EPS = 1e-6
LAM_INIT = 0.8 - 0.6 * math.exp(-0.3 * 0)

LANES = 128
VMEM_LIMIT = 56 * 1024 * 1024

REC_CHUNK = 64
REC_LEVELS = (1, 2, 4, 8, 16, 32)


def _cparams(sem, vmem=VMEM_LIMIT):
    return pltpu.CompilerParams(dimension_semantics=sem, vmem_limit_bytes=vmem)


def _dot(a, b):
    return jnp.dot(a, b, preferred_element_type=F32)


def _dot_nt(a, b):
    return lax.dot_general(a, b, (((1,), (1,)), ((), ())), preferred_element_type=F32)


def _dot_tn(a, b):
    return lax.dot_general(a, b, (((0,), (0,)), ((), ())), preferred_element_type=F32)


def _rms(x, g):
    ms = jnp.mean(x * x, axis=-1, keepdims=True)
    return x * lax.rsqrt(ms + NORM_EPS) * g


def _proj_prologue(x_ref, g_ref, h_ref):
    @pl.when(pl.program_id(1) == 0)
    def _():
        h_ref[...] = _rms(x_ref[...], g_ref[...]).astype(BF16)


def _proj_qT_kernel(x_ref, g_ref, wT_ref, qg_ref, o_ref, h_ref):
    _proj_prologue(x_ref, g_ref, h_ref)
    acc = _dot_nt(wT_ref[...], h_ref[...])
    tn, tm = acc.shape
    a3 = acc.reshape(tn // QK_DIM, QK_DIM, tm)
    ms = jnp.mean(a3 * a3, axis=1, keepdims=True)
    qn = (a3 * lax.rsqrt(ms + NORM_EPS)).reshape(tn, tm) * qg_ref[...]
    o_ref[0] = qn.astype(o_ref.dtype)


def _proj_k_kernel(x_ref, g_ref, w_ref, kg_ref, bd_ref, o_ref, h_ref):
    _proj_prologue(x_ref, g_ref, h_ref)
    acc = _dot(h_ref[...], w_ref[...])
    ss = _dot((acc * acc).astype(BF16), bd_ref[...])
    kn = acc * lax.rsqrt(ss * (1.0 / QK_DIM) + NORM_EPS) * kg_ref[...]
    o_ref[...] = kn.astype(o_ref.dtype)


def _proj_vT_kernel(x_ref, g_ref, wT_ref, o_ref, h_ref):
    _proj_prologue(x_ref, g_ref, h_ref)
    o_ref[0] = _dot_nt(wT_ref[...], h_ref[...]).astype(o_ref.dtype)


def _proj_rec_kernel(x_ref, g_ref, w_ref, o_ref, h_ref, *, blocks_per_seg):
    _proj_prologue(x_ref, g_ref, h_ref)
    acc = _dot(h_ref[...], w_ref[...])
    seg = pl.program_id(1) // blocks_per_seg
    is_silu = jnp.logical_or(seg == 0, seg == 4)

    @pl.when(is_silu)
    def _():
        o_ref[...] = acc * (1.0 / (1.0 + jnp.exp(-acc)))

    @pl.when(jnp.logical_not(is_silu))
    def _():
        o_ref[...] = acc


def _projections(x2d, norm_g, w_in, q_norm_g, k_norm_g, tm):
    n_tok = x2d.shape[0]
    nb = n_tok // tm
    g_row = norm_g.reshape(1, D_MODEL)
    w_bf = w_in.astype(BF16)
    x_spec = pl.BlockSpec((tm, D_MODEL), lambda i, j: (i, 0))
    g_spec = pl.BlockSpec((1, D_MODEL), lambda i, j: (0, 0))
    scratch = [pltpu.VMEM((tm, D_MODEL), BF16)]
    sem = ("parallel", "arbitrary")

    tn = 512
    wqT = w_bf[:, 0:ATTN_W].T
    qg_col = jnp.tile(q_norm_g.astype(F32) * (QK_DIM ** -0.5), 2 * HEADS).reshape(ATTN_W, 1)
    qT = pl.pallas_call(
        _proj_qT_kernel,
        grid=(nb, ATTN_W // tn),
        in_specs=[x_spec, g_spec,
                  pl.BlockSpec((tn, D_MODEL), lambda i, j: (j, 0)),
                  pl.BlockSpec((tn, 1), lambda i, j: (j, 0))],
        out_specs=pl.BlockSpec((1, tn, tm), lambda i, j: (i, j, 0)),
        out_shape=jax.ShapeDtypeStruct((nb, ATTN_W, tm), BF16),
        scratch_shapes=scratch, compiler_params=_cparams(sem), name="proj_qT",
    )(x2d, g_row, wqT, qg_col)

    tnk = 256
    kg_row = jnp.tile(k_norm_g.astype(F32), 2 * HEADS).reshape(1, ATTN_W)
    grp = np.arange(tnk) // QK_DIM
    bd = jnp.asarray((grp[:, None] == grp[None, :]).astype(np.float32), dtype=BF16)
    k = pl.pallas_call(
        _proj_k_kernel,
        grid=(nb, ATTN_W // tnk),
        in_specs=[x_spec, g_spec,
                  pl.BlockSpec((D_MODEL, tnk), lambda i, j: (0, j)),
                  pl.BlockSpec((1, tnk), lambda i, j: (0, j)),
                  pl.BlockSpec((tnk, tnk), lambda i, j: (0, 0))],
        out_specs=pl.BlockSpec((tm, tnk), lambda i, j: (i, j)),
        out_shape=jax.ShapeDtypeStruct((n_tok, ATTN_W), BF16),
        scratch_shapes=scratch, compiler_params=_cparams(sem), name="proj_k",
    )(x2d, g_row, w_bf[:, ATTN_W:2 * ATTN_W], kg_row, bd)

    wvT = w_bf[:, 2 * ATTN_W:3 * ATTN_W].T
    vT = pl.pallas_call(
        _proj_vT_kernel,
        grid=(nb, ATTN_W // tn),
        in_specs=[x_spec, g_spec, pl.BlockSpec((tn, D_MODEL), lambda i, j: (j, 0))],
        out_specs=pl.BlockSpec((1, tn, tm), lambda i, j: (i, j, 0)),
        out_shape=jax.ShapeDtypeStruct((nb, ATTN_W, tm), BF16),
        scratch_shapes=scratch, compiler_params=_cparams(sem), name="proj_vT",
    )(x2d, g_row, wvT)

    rec_w = 5 * REC_W
    rec = pl.pallas_call(
        functools.partial(_proj_rec_kernel, blocks_per_seg=REC_W // tn),
        grid=(nb, rec_w // tn),
        in_specs=[x_spec, g_spec, pl.BlockSpec((D_MODEL, tn), lambda i, j: (0, j))],
        out_specs=pl.BlockSpec((tm, tn), lambda i, j: (i, j)),
        out_shape=jax.ShapeDtypeStruct((n_tok, rec_w), F32),
        scratch_shapes=scratch, compiler_params=_cparams(sem), name="proj_rec",
    )(x2d, g_row, w_bf[:, 3 * ATTN_W:])
    return qT, k, vT, rec


def _attn_kernel(slopes_ref, qT_ref, k_ref, vT_ref, lam_ref, g_ref, o_ref,
                 m_ref, l_ref, acc_ref, *, nk, tk, tq):
    h = pl.program_id(1)
    i = pl.program_id(2)
    slope = slopes_ref[h]
    qT = qT_ref[0]
    zero_half = jnp.zeros((QK_DIM, tq), qT.dtype)
    q_maps = (jnp.concatenate([qT[:QK_DIM], zero_half], axis=0),
              jnp.concatenate([zero_half, qT[QK_DIM:]], axis=0))
    rel = (lax.broadcasted_iota(I32, (tk, tq), 1) - lax.broadcasted_iota(I32, (tk, tq), 0))
    m_ref[...] = jnp.full(m_ref.shape, -jnp.inf, F32)
    l_ref[...] = jnp.zeros(l_ref.shape, F32)
    acc_ref[...] = jnp.zeros(acc_ref.shape, F32)

    def body(j, carry):
        kblk = k_ref[pl.ds(pl.multiple_of(j * tk, tk), tk), :]
        vblk = vT_ref[j]
        bias = slope * jnp.abs(rel + (i * tq - j * tk)).astype(F32)
        for c in range(2):
            s = _dot(kblk, q_maps[c]) - bias
            m_old = m_ref[c]
            m_new = jnp.maximum(m_old, jnp.max(s, axis=0, keepdims=True))
            p = jnp.exp(s - m_new)
            alpha = jnp.exp(m_old - m_new)
            l_ref[c] = alpha * l_ref[c] + jnp.sum(p, axis=0, keepdims=True)
            acc_ref[c] = alpha * acc_ref[c] + _dot(vblk, p.astype(BF16))
            m_ref[c] = m_new
        return carry

    lax.fori_loop(0, nk, body, 0)

    lp = lam_ref[...]
    lam = (jnp.exp(jnp.sum(lp[0:1] * lp[1:2], axis=-1, keepdims=True))
           - jnp.exp(jnp.sum(lp[2:3] * lp[3:4], axis=-1, keepdims=True)) + LAM_INIT)
    o = acc_ref[0] / l_ref[0] - lam * (acc_ref[1] / l_ref[1])
    ms = jnp.mean(o * o, axis=0, keepdims=True)
    on = o * lax.rsqrt(ms + NORM_EPS) * g_ref[...] * (1.0 - LAM_INIT)
    o_ref[...] = on.T.astype(o_ref.dtype)


def _attention(qT, k, vT, diff_lambda, diff_norm_g, batch, seq, tq):
    nq = seq // tq
    slopes = jnp.exp2(-8.0 * jnp.arange(1, HEADS + 1, dtype=F32) / HEADS)
    g_col = diff_norm_g.astype(F32).reshape(ATTN_W, 1)
    grid_spec = pltpu.PrefetchScalarGridSpec(
        num_scalar_prefetch=1,
        grid=(batch, HEADS, nq),
        in_specs=[
            pl.BlockSpec((1, HEAD_W, tq), lambda b, h, i, s: (b * nq + i, h, 0)),
            pl.BlockSpec((seq, HEAD_W), lambda b, h, i, s: (b, h)),
            pl.BlockSpec((nq, HEAD_W, tq), lambda b, h, i, s: (b, h, 0)),
            pl.BlockSpec((4, QK_DIM), lambda b, h, i, s: (0, 0)),
            pl.BlockSpec((HEAD_W, 1), lambda b, h, i, s: (h, 0)),
        ],
        out_specs=pl.BlockSpec((tq, HEAD_W), lambda b, h, i, s: (b * nq + i, h)),
        scratch_shapes=[pltpu.VMEM((2, 1, tq), F32), pltpu.VMEM((2, 1, tq), F32),
                        pltpu.VMEM((2, HEAD_W, tq), F32)],
    )
    return pl.pallas_call(
        functools.partial(_attn_kernel, nk=nq, tk=tq, tq=tq),
        grid_spec=grid_spec,
        out_shape=jax.ShapeDtypeStruct((batch * seq, ATTN_W), BF16),
        compiler_params=_cparams(("parallel", "parallel", "arbitrary")), name="diff_attn",
    )(slopes, qT, k, vT, diff_lambda.astype(F32), g_col)


def _rec_mats(reverse):
    c = REC_CHUNK
    t = np.arange(c)[:, None]
    u = np.arange(c)[None, :]
    mats = [(u >= t) if reverse else (u <= t)]
    for lv in REC_LEVELS:
        if reverse:
            bd = (t // (2 * lv)) * (2 * lv) + lv
            mats.append(u >= bd)
        else:
            bd = (t // (2 * lv)) * (2 * lv) + lv - 1
            mats.append(u <= bd)
    return jnp.asarray(np.concatenate(mats, axis=0).astype(np.float32))


def _rec_chunk(q, z, v, lb, st, mats, reverse):
    c = REC_CHUNK
    sig = 1.0 / (1.0 + jnp.exp(-z))
    g = jnp.log(lb + (1.0 - lb) * sig)
    kk = (1.0 - lb) * (1.0 / (1.0 + jnp.exp(z)))
    stacked = jnp.dot(mats, g, precision=lax.Precision.HIGHEST, preferred_element_type=F32)
    b = stacked[0:c]
    row = lax.broadcasted_iota(I32, (c, HEAD_W), 0)
    ti = lax.broadcasted_iota(I32, (c, c), 0)
    si = lax.broadcasted_iota(I32, (c, c), 1)
    a = jnp.where(ti == si, _dot_nt(q.astype(BF16), kk.astype(BF16)), 0.0)
    for n, lv in enumerate(REC_LEVELS):
        r = stacked[(n + 1) * c:(n + 2) * c]
        e = jnp.exp(-jnp.abs(b - r))
        in_upper = (row & (2 * lv - 1)) >= lv
        is_query = jnp.logical_not(in_upper) if reverse else in_upper
        qt = jnp.where(is_query, q * e, 0.0).astype(BF16)
        kt = jnp.where(is_query, 0.0, kk * e).astype(BF16)
        shift = (2 * lv).bit_length() - 1
        same = (ti >> shift) == (si >> shift)
        a = a + jnp.where(same, _dot_nt(qt, kt), 0.0)
    b_end = b[0:1] if reverse else b[c - 1:c]
    o = _dot(a.astype(BF16), v.astype(BF16)) + _dot_nt((q * jnp.exp(b)).astype(BF16), st.astype(BF16))
    kd = (kk * jnp.exp(b_end - b)).astype(BF16)
    st_new = st * jnp.exp(b_end) + _dot_tn(v.astype(BF16), kd)
    return o, st_new


def _rec_kernel(q_ref, z_ref, v_ref, lbp_ref, mats_ref, o_ref, st_ref, *, reverse, n_chunks):
    @pl.when(pl.program_id(2) == 0)
    def _():
        st_ref[...] = jnp.zeros(st_ref.shape, F32)

    lbp = lbp_ref[...]
    mx = jnp.maximum(lbp[0:1], lbp[1:2])
    e0 = jnp.exp(lbp[0:1] - mx)
    e1 = jnp.exp(lbp[1:2] - mx)
    lb = e0 / (e0 + e1)
    mats = mats_ref[...]
    st = st_ref[...]
    order = range(n_chunks - 1, -1, -1) if reverse else range(n_chunks)
    for cidx in order:
        sl = slice(cidx * REC_CHUNK, (cidx + 1) * REC_CHUNK)
        o, st = _rec_chunk(q_ref[sl, :], z_ref[sl, :], v_ref[sl, :], lb, st, mats, reverse)
        o_ref[sl, :] = o
    st_ref[...] = st


def _recurrence(rec, rec_lower_bound, batch, seq, tb, reverse):
    nb = seq // tb
    d = 1 if reverse else 0
    mats = _rec_mats(reverse)
    lbp = rec_lower_bound[d].astype(F32)

    def blk(b, i):
        return b * nb + (nb - 1 - i if reverse else i)

    return pl.pallas_call(
        functools.partial(_rec_kernel, reverse=reverse, n_chunks=tb // REC_CHUNK),
        grid=(batch, HEADS, nb),
        in_specs=[
            pl.BlockSpec((tb, HEAD_W), lambda b, h, i: (blk(b, i), h)),
            pl.BlockSpec((tb, HEAD_W), lambda b, h, i: (blk(b, i), HEADS * (1 + d) + h)),
            pl.BlockSpec((tb, HEAD_W), lambda b, h, i: (blk(b, i), HEADS * 3 + h)),
            pl.BlockSpec((2, HEAD_W), lambda b, h, i: (0, h)),
            pl.BlockSpec(mats.shape, lambda b, h, i: (0, 0)),
        ],
        out_specs=pl.BlockSpec((tb, HEAD_W), lambda b, h, i: (blk(b, i), h)),
        out_shape=jax.ShapeDtypeStruct((batch * seq, REC_W), F32),
        scratch_shapes=[pltpu.VMEM((HEAD_W, HEAD_W), F32)],
        compiler_params=_cparams(("parallel", "parallel", "arbitrary")),
        name="rec_bwd" if reverse else "rec_fwd",
    )(rec, rec, rec, lbp, mats)


def _split_bf16(x):
    hi = x.astype(BF16)
    lo = (x - hi.astype(F32)).astype(BF16)
    return hi, lo


def _outproj_kernel(x_ref, ao_ref, of_ref, ob_ref, rg_ref, rng_ref, woa_ref, wor_ref,
                    fg_ref, wrh_ref, wrl_ref, x1_ref, aff_ref):
    parts = []
    for hd in range(HEADS):
        sl = slice(hd * HEAD_W, (hd + 1) * HEAD_W)
        o = of_ref[:, sl] + ob_ref[:, sl]
        parts.append((_rms(o, rng_ref[:, sl]) * rg_ref[:, sl]).astype(BF16))
    ro = jnp.concatenate(parts, axis=-1)
    x1 = x_ref[...] + _dot(ao_ref[...], woa_ref[...]) + _dot(ro, wor_ref[...])
    x1_ref[...] = x1
    h2 = _rms(x1, fg_ref[...])
    hi, lo = _split_bf16(h2)
    logits = _dot(hi, wrh_ref[...]) + _dot(lo, wrh_ref[...]) + _dot(hi, wrl_ref[...])
    lane = lax.broadcasted_iota(I32, logits.shape, 1)
    logits = jnp.where(lane < N_EXPERTS, logits, -jnp.inf)
    mx = jnp.max(logits, axis=-1, keepdims=True)
    ex = jnp.exp(logits - mx)
    aff_ref[...] = ex / jnp.sum(ex, axis=-1, keepdims=True)


def _outproj(x2d, ao, o_f, o_b, rec, rec_norm_g, w_out, norm_ffn_g, w_router, tm):
    n_tok = x2d.shape[0]
    w_bf = w_out.astype(BF16)
    wr = jnp.pad(w_router.astype(F32), ((0, 0), (0, LANES - N_EXPERTS)))
    wr_hi = wr.astype(BF16)
    wr_lo = (wr - wr_hi.astype(F32)).astype(BF16)
    row = lambda i: (i, 0)
    fixed = lambda i: (0, 0)
    return pl.pallas_call(
        _outproj_kernel,
        grid=(n_tok // tm,),
        in_specs=[
            pl.BlockSpec((tm, D_MODEL), row),
            pl.BlockSpec((tm, ATTN_W), row),
            pl.BlockSpec((tm, REC_W), row),
            pl.BlockSpec((tm, REC_W), row),
            pl.BlockSpec((tm, REC_W), lambda i: (i, 4)),
            pl.BlockSpec((1, REC_W), fixed),
            pl.BlockSpec((ATTN_W, D_MODEL), fixed),
            pl.BlockSpec((REC_W, D_MODEL), fixed),
            pl.BlockSpec((1, D_MODEL), fixed),
            pl.BlockSpec((D_MODEL, LANES), fixed),
            pl.BlockSpec((D_MODEL, LANES), fixed),
        ],
        out_specs=[pl.BlockSpec((tm, D_MODEL), row), pl.BlockSpec((tm, LANES), row)],
        out_shape=[jax.ShapeDtypeStruct((n_tok, D_MODEL), F32),
                   jax.ShapeDtypeStruct((n_tok, LANES), F32)],
        compiler_params=_cparams(("parallel",)), name="outproj_router",
    )(x2d, ao, o_f, o_b, rec, rec_norm_g.astype(F32).reshape(1, REC_W),
      w_bf[:ATTN_W], w_bf[ATTN_W:], norm_ffn_g.astype(F32).reshape(1, D_MODEL), wr_hi, wr_lo)


TOPK_BLK = 256


def _topk_kernel(aff_ref, tri_ref, idx_ref, gate_ref, pos_ref, *, n_tok, cap):
    nblk = n_tok // TOPK_BLK
    bits = pltpu.bitcast(aff_ref[...], I32)

    def count_ge(trial):
        return jnp.sum((bits >= trial).astype(I32), axis=0, keepdims=True)

    def search(step, cand):
        trial = cand | (jnp.int32(1) << (30 - step))
        return jnp.where(count_ge(trial) >= cap, trial, cand)

    thr = lax.fori_loop(0, 31, search, jnp.zeros((1, LANES), I32))
    gt = bits > thr
    tie = bits == thr
    need = (cap - jnp.sum(gt.astype(I32), axis=0, keepdims=True)).astype(F32)

    def running_count_in_place():
        def blk(j, carry):
            sl = pl.ds(pl.multiple_of(j * TOPK_BLK, TOPK_BLK), TOPK_BLK)
            pre = _dot(tri_ref[...], pos_ref[sl, :].astype(BF16)) + carry
            pos_ref[sl, :] = pre
            return pre[TOPK_BLK - 1:TOPK_BLK]
        lax.fori_loop(0, nblk, blk, jnp.zeros((1, LANES), F32))

    pos_ref[...] = tie.astype(F32)
    running_count_in_place()
    sel = jnp.logical_or(gt, jnp.logical_and(tie, pos_ref[...] <= need))
    pos_ref[...] = sel.astype(F32)
    running_count_in_place()
    pos_ref[...] = jnp.where(sel, pos_ref[...], 0.0)

    slot = (lax.broadcasted_iota(I32, (1, cap), 1) + 1).astype(F32)
    tok = lax.broadcasted_iota(I32, (TOPK_BLK, 1), 0)
    for e in range(N_EXPERTS):
        def blk(j, carry):
            acc_i, acc_g = carry
            sl = pl.ds(pl.multiple_of(j * TOPK_BLK, TOPK_BLK), TOPK_BLK)
            hit = pos_ref[sl, e:e + 1] == slot
            acc_i = acc_i + jnp.sum(jnp.where(hit, tok + j * TOPK_BLK, 0), axis=0, keepdims=True)
            acc_g = acc_g + jnp.sum(jnp.where(hit, aff_ref[sl, e:e + 1], 0.0), axis=0, keepdims=True)
            return acc_i, acc_g
        acc_i, acc_g = lax.fori_loop(0, nblk, blk,
                                     (jnp.zeros((1, cap), I32), jnp.zeros((1, cap), F32)))
        idx_ref[e:e + 1, :] = acc_i
        gate_ref[e:e + 1, :] = acc_g


def _topk(aff, cap):
    n_tok = aff.shape[0]
    r = np.arange(TOPK_BLK)
    tri = jnp.asarray((r[None, :] <= r[:, None]).astype(np.float32), dtype=BF16)
    return pl.pallas_call(
        functools.partial(_topk_kernel, n_tok=n_tok, cap=cap),
        out_shape=[jax.ShapeDtypeStruct((N_EXPERTS, cap), I32),
                   jax.ShapeDtypeStruct((N_EXPERTS, cap), F32)],
        scratch_shapes=[pltpu.VMEM((n_tok, LANES), F32)],
        compiler_params=pltpu.CompilerParams(vmem_limit_bytes=VMEM_LIMIT), name="expert_topk",
    )(aff, tri)


def _row_copies_start(src_hbm, dst_vmem, idx_ref, e, cap, sem, gather):
    def issue(p, carry):
        t = idx_ref[e, p]
        if gather:
            pltpu.make_async_copy(src_hbm.at[pl.ds(t, 1), :], dst_vmem.at[pl.ds(p, 1), :], sem).start()
        else:
            pltpu.make_async_copy(dst_vmem.at[pl.ds(p, 1), :], src_hbm.at[pl.ds(t, 1), :], sem).start()
        return carry
    lax.fori_loop(0, cap, issue, 0)


def _row_copies_wait(hbm, vmem, cap, sem, gather):
    if gather:
        pltpu.make_async_copy(hbm.at[pl.ds(0, cap), :], vmem, sem).wait()
    else:
        pltpu.make_async_copy(vmem, hbm.at[pl.ds(0, cap), :], sem).wait()


def _moe_kernel(idx_ref, gate_ref, fg_ref, x_hbm, acc_in_hbm, wg_ref, wu_ref, wd_ref, out_hbm,
                rows, xe, ye, sem, *, cap, nf):
    del acc_in_hbm
    e = pl.program_id(0)
    f = pl.program_id(1)

    @pl.when(f == 0)
    def _():
        _row_copies_start(x_hbm, rows, idx_ref, e, cap, sem, True)
        _row_copies_wait(x_hbm, rows, cap, sem, True)
        xe[...] = _rms(rows[...], fg_ref[...]).astype(BF16)
        ye[...] = jnp.zeros(ye.shape, F32)

    xb = xe[...]
    hg = _dot(xb, wg_ref[0].astype(BF16))
    hu = _dot(xb, wu_ref[0].astype(BF16))
    hid = hg * (1.0 / (1.0 + jnp.exp(-hg))) * hu
    ye[...] += _dot(hid.astype(BF16), wd_ref[0].astype(BF16))

    @pl.when(f == nf - 1)
    def _():
        _row_copies_start(out_hbm, rows, idx_ref, e, cap, sem, True)
        _row_copies_wait(out_hbm, rows, cap, sem, True)
        rows[...] = rows[...] + ye[...] * gate_ref[0]
        _row_copies_start(out_hbm, rows, idx_ref, e, cap, sem, False)
        _row_copies_wait(out_hbm, rows, cap, sem, False)


def _moe(x1, idx, gates, norm_ffn_g, w_gate, w_up, w_down, tf):
    n_tok = x1.shape[0]
    cap = idx.shape[1]
    nf = D_EXPERT // tf
    grid_spec = pltpu.PrefetchScalarGridSpec(
        num_scalar_prefetch=1,
        grid=(N_EXPERTS, nf),
        in_specs=[
            pl.BlockSpec((1, cap, 1), lambda e, f, s: (e, 0, 0)),
            pl.BlockSpec((1, D_MODEL), lambda e, f, s: (0, 0)),
            pl.BlockSpec(memory_space=pl.ANY),
            pl.BlockSpec(memory_space=pl.ANY),
            pl.BlockSpec((1, D_MODEL, tf), lambda e, f, s: (e, 0, f)),
            pl.BlockSpec((1, D_MODEL, tf), lambda e, f, s: (e, 0, f)),
            pl.BlockSpec((1, tf, D_MODEL), lambda e, f, s: (e, f, 0)),
        ],
        out_specs=pl.BlockSpec(memory_space=pl.ANY),
        scratch_shapes=[pltpu.VMEM((cap, D_MODEL), F32), pltpu.VMEM((cap, D_MODEL), BF16),
                        pltpu.VMEM((cap, D_MODEL), F32), pltpu.SemaphoreType.DMA(())],
    )
    return pl.pallas_call(
        functools.partial(_moe_kernel, cap=cap, nf=nf),
        grid_spec=grid_spec,
        out_shape=jax.ShapeDtypeStruct((n_tok, D_MODEL), F32),
        input_output_aliases={4: 0},
        compiler_params=_cparams(("arbitrary", "arbitrary")), name="expert_ffn",
    )(idx, gates.reshape(N_EXPERTS, cap, 1), norm_ffn_g.astype(F32).reshape(1, D_MODEL),
      x1, x1, w_gate, w_up, w_down)


def _ple_kernel(x_ref, p_ref, g_ref, wg_ref, wp_ref, o_ref):
    x = x_ref[...]
    zg = _dot(_rms(x, g_ref[...]).astype(BF16), wg_ref[...])
    gate = 1.0 / (1.0 + jnp.exp(-zg))
    o_ref[...] = x + gate * _dot(p_ref[...].astype(BF16), wp_ref[...])


def _ple(x2, p2d, norm_ple_g, w_ple_gate, w_ple_proj, tm):
    n_tok = x2.shape[0]
    row = lambda i: (i, 0)
    fixed = lambda i: (0, 0)
    return pl.pallas_call(
        _ple_kernel,
        grid=(n_tok // tm,),
        in_specs=[pl.BlockSpec((tm, D_MODEL), row), pl.BlockSpec((tm, PLE_DIM), row),
                  pl.BlockSpec((1, D_MODEL), fixed), pl.BlockSpec((D_MODEL, D_MODEL), fixed),
                  pl.BlockSpec((PLE_DIM, D_MODEL), fixed)],
        out_specs=pl.BlockSpec((tm, D_MODEL), row),
        out_shape=jax.ShapeDtypeStruct((n_tok, D_MODEL), F32),
        compiler_params=_cparams(("parallel",)), name="ple_gate",
    )(x2, p2d, norm_ple_g.astype(F32).reshape(1, D_MODEL), w_ple_gate.astype(BF16),
      w_ple_proj.astype(BF16))


def _tiles(seq):
    t_attn = min(512, seq)
    t_rec = min(256, seq)
    t_row = min(256, seq)
    return t_attn, t_rec, t_row


def _layer(x, p, norm_mix_g, w_in, q_norm_g, k_norm_g, diff_lambda, diff_norm_g, rec_lower_bound,
           rec_norm_g, w_out, norm_ffn_g, w_router, w_expert_gate, w_expert_up, w_expert_down,
           norm_ple_g, w_ple_gate, w_ple_proj):
    batch, seq, _ = x.shape
    n_tok = batch * seq
    t_attn, t_rec, t_row = _tiles(seq)
    x2d = x.reshape(n_tok, D_MODEL)
    qT, k, vT, rec = _projections(x2d, norm_mix_g[0], w_in[0], q_norm_g[0], k_norm_g[0], t_attn)
    ao = _attention(qT, k, vT, diff_lambda[0], diff_norm_g[0], batch, seq, t_attn)
    o_f = _recurrence(rec, rec_lower_bound, batch, seq, t_rec, False)
    o_b = _recurrence(rec, rec_lower_bound, batch, seq, t_rec, True)
    x1, aff = _outproj(x2d, ao, o_f, o_b, rec, rec_norm_g[0], w_out[0], norm_ffn_g[0],
                       w_router[0], t_row)
    cap = max(1, (CAPACITY_FACTOR * n_tok) // N_EXPERTS)
    idx, gates = _topk(aff, cap)
    x2 = _moe(x1, idx, gates, norm_ffn_g[0], w_expert_gate[0], w_expert_up[0], w_expert_down[0],
              tf=256)
    y = _ple(x2, p[0].reshape(n_tok, PLE_DIM), norm_ple_g[0], w_ple_gate[0], w_ple_proj[0], t_row)
    return y.reshape(x.shape)


def kernel(x_prompt, x_sample, p_prompt, p_sample, norm_mix_g, w_in, q_norm_g, k_norm_g, diff_lambda, diff_norm_g, rec_lower_bound, rec_norm_g, w_out, norm_ffn_g, w_router, w_expert_gate, w_expert_up, w_expert_down, norm_ple_g, w_ple_gate, w_ple_proj):
    weights = (norm_mix_g, w_in, q_norm_g, k_norm_g, diff_lambda, diff_norm_g, rec_lower_bound,
               rec_norm_g, w_out, norm_ffn_g, w_router, w_expert_gate, w_expert_up, w_expert_down,
               norm_ple_g, w_ple_gate, w_ple_proj)
    return (_layer(x_prompt, p_prompt, *weights), _layer(x_sample, p_sample, *weights))
```

```python
import functools
import math

import numpy as np
import jax
import jax.numpy as jnp
from jax import lax
from jax.experimental import pallas as pl
from jax.experimental.pallas import tpu as pltpu

F32 = jnp.float32
BF16 = jnp.bfloat16
I32 = jnp.int32

D_MODEL = 2048
PLE_DIM = 256
HEADS = 8
QK_DIM = 64
V_DIM = 128
HEAD_W = 128
ATTN_W = HEADS * V_DIM
REC_W = HEADS * V_DIM
N_EXPERTS = 16
CAPACITY_FACTOR = 2
D_EXPERT = 2048
NORM_EPS = 1e-6
LAM_INIT = 0.8 - 0.6 * math.exp(-0.3 * 0)

LANES = 128
LOG2E = 1.4426950408889634
VMEM_LIMIT = 56 * 1024 * 1024

REC_CHUNK = 64
REC_LEVELS = (1, 2, 4, 8, 16, 32)


def _cparams(sem, vmem=VMEM_LIMIT):
    return pltpu.CompilerParams(dimension_semantics=sem, vmem_limit_bytes=vmem)


def _dot(a, b):
    return jnp.dot(a, b, preferred_element_type=F32)


def _dot_nt(a, b):
    return lax.dot_general(a, b, (((1,), (1,)), ((), ())), preferred_element_type=F32)


def _dot_tn(a, b):
    return lax.dot_general(a, b, (((0,), (0,)), ((), ())), preferred_element_type=F32)


def _rms(x, g):
    ms = jnp.mean(x * x, axis=-1, keepdims=True)
    return x * lax.rsqrt(ms + NORM_EPS) * g


def _proj_prologue(x_ref, g_ref, h_ref):
    @pl.when(pl.program_id(1) == 0)
    def _():
        h_ref[...] = _rms(x_ref[...], g_ref[...]).astype(BF16)


def _proj_qT_kernel(x_ref, g_ref, wT_ref, qg_ref, o_ref, h_ref):
    _proj_prologue(x_ref, g_ref, h_ref)
    acc = _dot_nt(wT_ref[...], h_ref[...])
    tn, tm = acc.shape
    a3 = acc.reshape(tn // QK_DIM, QK_DIM, tm)
    ms = jnp.mean(a3 * a3, axis=1, keepdims=True)
    qn = (a3 * lax.rsqrt(ms + NORM_EPS)).reshape(tn, tm) * qg_ref[...]
    o_ref[0] = qn.astype(o_ref.dtype)


def _proj_k_kernel(x_ref, g_ref, w_ref, kg_ref, bd_ref, o_ref, h_ref):
    _proj_prologue(x_ref, g_ref, h_ref)
    acc = _dot(h_ref[...], w_ref[...])
    ss = _dot((acc * acc).astype(BF16), bd_ref[...])
    kn = acc * lax.rsqrt(ss * (1.0 / QK_DIM) + NORM_EPS) * kg_ref[...]
    o_ref[...] = kn.astype(o_ref.dtype)


def _proj_vT_kernel(x_ref, g_ref, wT_ref, o_ref, h_ref):
    _proj_prologue(x_ref, g_ref, h_ref)
    o_ref[0] = _dot_nt(wT_ref[...], h_ref[...]).astype(o_ref.dtype)


def _proj_rec_kernel(x_ref, g_ref, w_ref, o_ref, h_ref, *, blocks_per_seg):
    _proj_prologue(x_ref, g_ref, h_ref)
    acc = _dot(h_ref[...], w_ref[...])
    seg = pl.program_id(1) // blocks_per_seg
    is_silu = jnp.logical_or(seg == 0, seg == 4)

    @pl.when(is_silu)
    def _():
        o_ref[...] = acc * (1.0 / (1.0 + jnp.exp(-acc)))

    @pl.when(jnp.logical_not(is_silu))
    def _():
        o_ref[...] = acc


def _projections(x2d, norm_g, w_in, q_norm_g, k_norm_g, tm):
    n_tok = x2d.shape[0]
    nb = n_tok // tm
    g_row = norm_g.reshape(1, D_MODEL)
    w_bf = w_in.astype(BF16)
    x_spec = pl.BlockSpec((tm, D_MODEL), lambda i, j: (i, 0))
    g_spec = pl.BlockSpec((1, D_MODEL), lambda i, j: (0, 0))
    scratch = [pltpu.VMEM((tm, D_MODEL), BF16)]
    sem = ("parallel", "arbitrary")

    tn = 512
    wqT = w_bf[:, 0:ATTN_W].T
    qg_col = jnp.tile(q_norm_g.astype(F32) * (QK_DIM ** -0.5 * LOG2E), 2 * HEADS).reshape(ATTN_W, 1)
    qT = pl.pallas_call(
        _proj_qT_kernel,
        grid=(nb, ATTN_W // tn),
        in_specs=[x_spec, g_spec,
                  pl.BlockSpec((tn, D_MODEL), lambda i, j: (j, 0)),
                  pl.BlockSpec((tn, 1), lambda i, j: (j, 0))],
        out_specs=pl.BlockSpec((1, tn, tm), lambda i, j: (i, j, 0)),
        out_shape=jax.ShapeDtypeStruct((nb, ATTN_W, tm), BF16),
        scratch_shapes=scratch, compiler_params=_cparams(sem), name="proj_qT",
    )(x2d, g_row, wqT, qg_col)

    tnk = 256
    kg_row = jnp.tile(k_norm_g.astype(F32), 2 * HEADS).reshape(1, ATTN_W)
    grp = np.arange(tnk) // QK_DIM
    bd = jnp.asarray((grp[:, None] == grp[None, :]).astype(np.float32), dtype=BF16)
    k = pl.pallas_call(
        _proj_k_kernel,
        grid=(nb, ATTN_W // tnk),
        in_specs=[x_spec, g_spec,
                  pl.BlockSpec((D_MODEL, tnk), lambda i, j: (0, j)),
                  pl.BlockSpec((1, tnk), lambda i, j: (0, j)),
                  pl.BlockSpec((tnk, tnk), lambda i, j: (0, 0))],
        out_specs=pl.BlockSpec((tm, tnk), lambda i, j: (i, j)),
        out_shape=jax.ShapeDtypeStruct((n_tok, ATTN_W), BF16),
        scratch_shapes=scratch, compiler_params=_cparams(sem), name="proj_k",
    )(x2d, g_row, w_bf[:, ATTN_W:2 * ATTN_W], kg_row, bd)

    wvT = w_bf[:, 2 * ATTN_W:3 * ATTN_W].T
    vT = pl.pallas_call(
        _proj_vT_kernel,
        grid=(nb, ATTN_W // tn),
        in_specs=[x_spec, g_spec, pl.BlockSpec((tn, D_MODEL), lambda i, j: (j, 0))],
        out_specs=pl.BlockSpec((1, tn, tm), lambda i, j: (i, j, 0)),
        out_shape=jax.ShapeDtypeStruct((nb, ATTN_W, tm), BF16),
        scratch_shapes=scratch, compiler_params=_cparams(sem), name="proj_vT",
    )(x2d, g_row, wvT)

    rec_w = 5 * REC_W
    tnr = REC_W
    rec = pl.pallas_call(
        functools.partial(_proj_rec_kernel, blocks_per_seg=REC_W // tnr),
        grid=(nb, rec_w // tnr),
        in_specs=[x_spec, g_spec, pl.BlockSpec((D_MODEL, tnr), lambda i, j: (0, j))],
        out_specs=pl.BlockSpec((tm, tnr), lambda i, j: (i, j)),
        out_shape=jax.ShapeDtypeStruct((n_tok, rec_w), F32),
        scratch_shapes=scratch, compiler_params=_cparams(sem), name="proj_rec",
    )(x2d, g_row, w_bf[:, 3 * ATTN_W:])
    return qT, k, vT, rec


L_ROWS = 16
ATTN_COLS = 256


def _attn_kernel(slopes_ref, qT_ref, k_ref, vT_ref, lam_ref, g_ref, o_ref,
                 m_ref, acc_ref, bias_ref, ta_ref, tb_ref, mta_ref, mtb_ref, *, nk, tk, tq):
    h = pl.program_id(1)
    i = pl.program_id(2)
    sl = slopes_ref[h]

    @pl.when(i == 0)
    def _():
        rel = (lax.broadcasted_iota(I32, (tk, tq), 1)
               - lax.broadcasted_iota(I32, (tk, tq), 0)).astype(F32)
        sr = sl * rel
        sa = sl * jnp.abs(rel)
        bias_ref[0] = jnp.concatenate([sr, sr], axis=1)
        bias_ref[1] = jnp.concatenate([sa, sa], axis=1)
        bias_ref[2] = jnp.concatenate([-sr, -sr], axis=1)

    qT = qT_ref[0]
    zero_half = jnp.zeros((QK_DIM, tq), qT.dtype)
    q_cat = jnp.concatenate([jnp.concatenate([qT[:QK_DIM], zero_half], axis=0),
                             jnp.concatenate([zero_half, qT[QK_DIM:]], axis=0)], axis=1)
    ones = jnp.ones((L_ROWS, tk), BF16)
    m_ref[...] = jnp.full(m_ref.shape, -jnp.inf, F32)
    acc_ref[...] = jnp.zeros(acc_ref.shape, F32)

    def shift_of(j):
        gap = jnp.abs(jnp.full((1, 2 * tq), (i - j) * tq, I32)).astype(F32)
        return -sl * gap

    def scores(j, t_ref, mt_ref):
        kblk = k_ref[pl.ds(pl.multiple_of(j * tk, tk), tk), :]
        side = jnp.where(j < i, 0, jnp.where(j == i, 1, 2))
        t = _dot(kblk, q_cat) - bias_ref[side]
        t_ref[...] = t
        mt_ref[...] = jnp.max(t, axis=0, keepdims=True) + shift_of(j)

    def accumulate(j, t_ref, mt_ref):
        m_old = m_ref[...]
        m_new = jnp.maximum(m_old, mt_ref[...])
        p = jnp.exp2(t_ref[...] - (m_new - shift_of(j)))
        v_aug = jnp.concatenate([vT_ref[j], ones], axis=0)
        acc_ref[...] = jnp.exp2(m_old - m_new) * acc_ref[...] + _dot(v_aug, p.astype(BF16))
        m_ref[...] = m_new

    def pair(jj, carry):
        j = 2 * jj
        scores(j + 1, tb_ref, mtb_ref)
        accumulate(j, ta_ref, mta_ref)
        scores(jnp.minimum(j + 2, nk - 1), ta_ref, mta_ref)
        accumulate(j + 1, tb_ref, mtb_ref)
        return carry

    scores(0, ta_ref, mta_ref)
    lax.fori_loop(0, nk // 2, pair, 0)
    if nk % 2:
        accumulate(nk - 1, ta_ref, mta_ref)

    lp = lam_ref[...]
    lam = (jnp.exp(jnp.sum(lp[0:1] * lp[1:2], axis=-1, keepdims=True))
           - jnp.exp(jnp.sum(lp[2:3] * lp[3:4], axis=-1, keepdims=True)) + LAM_INIT)
    acc = acc_ref[...]
    a0 = acc[:, :tq]
    a1 = acc[:, tq:]
    o = (a0[:V_DIM] / a0[V_DIM:V_DIM + 1]
         - lam * (a1[:V_DIM] / a1[V_DIM:V_DIM + 1]))
    ms = jnp.mean(o * o, axis=0, keepdims=True)
    on = o * lax.rsqrt(ms + NORM_EPS) * g_ref[...] * (1.0 - LAM_INIT)
    o_ref[...] = on.T.astype(o_ref.dtype)


def _attention(qT, k, vT, diff_lambda, diff_norm_g, batch, seq, tq):
    nq = seq // tq
    slopes = LOG2E * jnp.exp2(-8.0 * jnp.arange(1, HEADS + 1, dtype=F32) / HEADS)
    g_col = diff_norm_g.astype(F32).reshape(ATTN_W, 1)
    grid_spec = pltpu.PrefetchScalarGridSpec(
        num_scalar_prefetch=1,
        grid=(batch, HEADS, nq),
        in_specs=[
            pl.BlockSpec((1, HEAD_W, tq), lambda b, h, i, s: (b * nq + i, h, 0)),
            pl.BlockSpec((seq, HEAD_W), lambda b, h, i, s: (b, h)),
            pl.BlockSpec((nq, HEAD_W, tq), lambda b, h, i, s: (b, h, 0)),
            pl.BlockSpec((4, QK_DIM), lambda b, h, i, s: (0, 0)),
            pl.BlockSpec((HEAD_W, 1), lambda b, h, i, s: (h, 0)),
        ],
        out_specs=pl.BlockSpec((tq, HEAD_W), lambda b, h, i, s: (b * nq + i, h)),
        scratch_shapes=[pltpu.VMEM((1, 2 * tq), F32),
                        pltpu.VMEM((V_DIM + L_ROWS, 2 * tq), F32),
                        pltpu.VMEM((3, tq, 2 * tq), F32),
                        pltpu.VMEM((tq, 2 * tq), F32), pltpu.VMEM((tq, 2 * tq), F32),
                        pltpu.VMEM((1, 2 * tq), F32), pltpu.VMEM((1, 2 * tq), F32)],
    )
    return pl.pallas_call(
        functools.partial(_attn_kernel, nk=nq, tk=tq, tq=tq),
        grid_spec=grid_spec,
        out_shape=jax.ShapeDtypeStruct((batch * seq, ATTN_W), BF16),
        compiler_params=_cparams(("arbitrary", "arbitrary", "arbitrary")), name="diff_attn",
    )(slopes, qT, k, vT, diff_lambda.astype(F32), g_col)


def _rec_mats(reverse):
    c = REC_CHUNK
    t = np.arange(c)[:, None]
    u = np.arange(c)[None, :]
    mats = [(u >= t) if reverse else (u <= t)]
    for lv in REC_LEVELS:
        if reverse:
            bd = (t // (2 * lv)) * (2 * lv) + lv
            mats.append(u >= bd)
        else:
            bd = (t // (2 * lv)) * (2 * lv) + lv - 1
            mats.append(u <= bd)
    return jnp.asarray(np.concatenate(mats, axis=0).astype(np.float32), dtype=BF16)


def _rec_kernel(q_ref, z_ref, v_ref, lbp_ref, mats_ref, o_ref, st_ref, *, reverse, n_chunks):
    c = REC_CHUNK

    @pl.when(pl.program_id(2) == 0)
    def _():
        st_ref[...] = jnp.zeros(st_ref.shape, F32)

    lbp = lbp_ref[...]
    mx = jnp.maximum(lbp[0:1], lbp[1:2])
    e0 = jnp.exp(lbp[0:1] - mx)
    e1 = jnp.exp(lbp[1:2] - mx)
    lb = e0 / (e0 + e1)

    rows = [slice(n * c, (n + 1) * c) for n in range(n_chunks)]
    qs = [q_ref[sl, :] for sl in rows]
    vs = [v_ref[sl, :].astype(BF16) for sl in rows]
    gs, kks = [], []
    for sl in rows:
        z = z_ref[sl, :]
        sig = 1.0 / (1.0 + jnp.exp(-z))
        gs.append(jnp.log(lb + (1.0 - lb) * sig))
        kks.append((1.0 - lb) * (1.0 / (1.0 + jnp.exp(z))))

    g_cat = jnp.concatenate(gs, axis=1)
    g1 = g_cat.astype(BF16)
    r1 = g_cat - g1.astype(F32)
    g2 = r1.astype(BF16)
    g3 = (r1 - g2.astype(F32)).astype(BF16)
    mats = mats_ref[...]
    stacked = (_dot(mats, g3) + _dot(mats, g2)) + _dot(mats, g1)

    row = lax.broadcasted_iota(I32, (c, HEAD_W), 0)
    ti = lax.broadcasted_iota(I32, (c, c), 0)
    si = lax.broadcasted_iota(I32, (c, c), 1)
    is_query, same = [], []
    for lv in REC_LEVELS:
        in_upper = (row & (2 * lv - 1)) >= lv
        is_query.append(jnp.logical_not(in_upper) if reverse else in_upper)
        shift = (2 * lv).bit_length() - 1
        same.append((ti >> shift) == (si >> shift))
    diag = ti == si

    bs, o_intra, incs, qes, decays = [], [], [], [], []
    for n in range(n_chunks):
        lanes = slice(n * HEAD_W, (n + 1) * HEAD_W)
        b = stacked[0:c, lanes]
        q, kk = qs[n], kks[n]
        a = jnp.where(diag, _dot_nt(q.astype(BF16), kk.astype(BF16)), 0.0)
        for m, lv in enumerate(REC_LEVELS):
            r = stacked[(m + 1) * c:(m + 2) * c, lanes]
            e = jnp.exp(-jnp.abs(b - r))
            qt = jnp.where(is_query[m], q * e, 0.0).astype(BF16)
            kt = jnp.where(is_query[m], 0.0, kk * e).astype(BF16)
            a = a + jnp.where(same[m], _dot_nt(qt, kt), 0.0)
        b_end = b[0:1] if reverse else b[c - 1:c]
        o_intra.append(_dot(a.astype(BF16), vs[n]))
        incs.append(_dot_tn(vs[n], (kk * jnp.exp(b_end - b)).astype(BF16)))
        qes.append((q * jnp.exp(b)).astype(BF16))
        decays.append(jnp.exp(b_end))

    st = st_ref[...]
    order = range(n_chunks - 1, -1, -1) if reverse else range(n_chunks)
    for n in order:
        o_ref[rows[n], :] = o_intra[n] + _dot_nt(qes[n], st.astype(BF16))
        st = st * decays[n] + incs[n]
    st_ref[...] = st


def _recurrence(rec, rec_lower_bound, batch, seq, tb, reverse):
    nb = seq // tb
    d = 1 if reverse else 0
    mats = _rec_mats(reverse)
    lbp = rec_lower_bound[d].astype(F32)

    def blk(b, i):
        return b * nb + (nb - 1 - i if reverse else i)

    return pl.pallas_call(
        functools.partial(_rec_kernel, reverse=reverse, n_chunks=tb // REC_CHUNK),
        grid=(batch, HEADS, nb),
        in_specs=[
            pl.BlockSpec((tb, HEAD_W), lambda b, h, i: (blk(b, i), h)),
            pl.BlockSpec((tb, HEAD_W), lambda b, h, i: (blk(b, i), HEADS * (1 + d) + h)),
            pl.BlockSpec((tb, HEAD_W), lambda b, h, i: (blk(b, i), HEADS * 3 + h)),
            pl.BlockSpec((2, HEAD_W), lambda b, h, i: (0, h)),
            pl.BlockSpec(mats.shape, lambda b, h, i: (0, 0)),
        ],
        out_specs=pl.BlockSpec((tb, HEAD_W), lambda b, h, i: (blk(b, i), h)),
        out_shape=jax.ShapeDtypeStruct((batch * seq, REC_W), F32),
        scratch_shapes=[pltpu.VMEM((HEAD_W, HEAD_W), F32)],
        compiler_params=_cparams(("parallel", "parallel", "arbitrary")),
        name="rec_bwd" if reverse else "rec_fwd",
    )(rec, rec, rec, lbp, mats)


def _split_bf16(x):
    hi = x.astype(BF16)
    lo = (x - hi.astype(F32)).astype(BF16)
    return hi, lo


def _outproj_kernel(x_ref, ao_ref, of_ref, ob_ref, rg_ref, rng_ref, woa_ref, wor_ref,
                    fg_ref, wrh_ref, wrl_ref, x1_ref, aff_ref):
    parts = []
    for hd in range(HEADS):
        sl = slice(hd * HEAD_W, (hd + 1) * HEAD_W)
        o = of_ref[:, sl] + ob_ref[:, sl]
        parts.append((_rms(o, rng_ref[:, sl]) * rg_ref[:, sl]).astype(BF16))
    ro = jnp.concatenate(parts, axis=-1)
    x1 = x_ref[...] + _dot(ao_ref[...], woa_ref[...]) + _dot(ro, wor_ref[...])
    x1_ref[...] = x1
    h2 = _rms(x1, fg_ref[...])
    hi, lo = _split_bf16(h2)
    logits = _dot(hi, wrh_ref[...]) + _dot(lo, wrh_ref[...]) + _dot(hi, wrl_ref[...])
    lane = lax.broadcasted_iota(I32, logits.shape, 1)
    logits = jnp.where(lane < N_EXPERTS, logits, -jnp.inf)
    mx = jnp.max(logits, axis=-1, keepdims=True)
    ex = jnp.exp(logits - mx)
    aff_ref[...] = ex / jnp.sum(ex, axis=-1, keepdims=True)


def _outproj(x2d, ao, o_f, o_b, rec, rec_norm_g, w_out, norm_ffn_g, w_router, tm):
    n_tok = x2d.shape[0]
    w_bf = w_out.astype(BF16)
    wr = jnp.pad(w_router.astype(F32), ((0, 0), (0, LANES - N_EXPERTS)))
    wr_hi = wr.astype(BF16)
    wr_lo = (wr - wr_hi.astype(F32)).astype(BF16)
    row = lambda i: (i, 0)
    fixed = lambda i: (0, 0)
    return pl.pallas_call(
        _outproj_kernel,
        grid=(n_tok // tm,),
        in_specs=[
            pl.BlockSpec((tm, D_MODEL), row),
            pl.BlockSpec((tm, ATTN_W), row),
            pl.BlockSpec((tm, REC_W), row),
            pl.BlockSpec((tm, REC_W), row),
            pl.BlockSpec((tm, REC_W), lambda i: (i, 4)),
            pl.BlockSpec((1, REC_W), fixed),
            pl.BlockSpec((ATTN_W, D_MODEL), fixed),
            pl.BlockSpec((REC_W, D_MODEL), fixed),
            pl.BlockSpec((1, D_MODEL), fixed),
            pl.BlockSpec((D_MODEL, LANES), fixed),
            pl.BlockSpec((D_MODEL, LANES), fixed),
        ],
        out_specs=[pl.BlockSpec((tm, D_MODEL), row), pl.BlockSpec((tm, LANES), row)],
        out_shape=[jax.ShapeDtypeStruct((n_tok, D_MODEL), F32),
                   jax.ShapeDtypeStruct((n_tok, LANES), F32)],
        compiler_params=_cparams(("parallel",)), name="outproj_router",
    )(x2d, ao, o_f, o_b, rec, rec_norm_g.astype(F32).reshape(1, REC_W),
      w_bf[:ATTN_W], w_bf[ATTN_W:], norm_ffn_g.astype(F32).reshape(1, D_MODEL), wr_hi, wr_lo)


TOPK_BLK = 256


def _topk_kernel(aff_ref, tri_ref, idx_ref, gate_ref, pos_ref, *, n_tok, cap):
    nblk = n_tok // TOPK_BLK
    aff = aff_ref[...]

    def search(step, cand):
        trial = cand | (jnp.int32(1) << (30 - step))
        n_ge = jnp.sum((aff >= pltpu.bitcast(trial, F32)).astype(I32), axis=0, keepdims=True)
        return jnp.where(n_ge >= cap, trial, cand)

    thr = pltpu.bitcast(lax.fori_loop(0, 31, search, jnp.zeros((1, LANES), I32)), F32)
    gt = aff > thr
    tie = aff == thr
    need = (cap - jnp.sum(gt.astype(I32), axis=0, keepdims=True)).astype(F32)

    def running_count_in_place():
        def blk(j, carry):
            sl = pl.ds(pl.multiple_of(j * TOPK_BLK, TOPK_BLK), TOPK_BLK)
            pre = _dot(tri_ref[...], pos_ref[sl, :].astype(BF16)) + carry
            pos_ref[sl, :] = pre
            return pre[TOPK_BLK - 1:TOPK_BLK]
        lax.fori_loop(0, nblk, blk, jnp.zeros((1, LANES), F32))

    pos_ref[...] = tie.astype(F32)
    running_count_in_place()
    sel = jnp.logical_or(gt, jnp.logical_and(tie, pos_ref[...] <= need))
    pos_ref[...] = sel.astype(F32)
    running_count_in_place()
    pos_ref[...] = jnp.where(sel, pos_ref[...], 0.0)

    slot = (lax.broadcasted_iota(I32, (1, cap), 1) + 1).astype(F32)
    tok = lax.broadcasted_iota(I32, (TOPK_BLK, 1), 0)
    for e in range(N_EXPERTS):
        def blk(j, carry):
            acc_i, acc_g = carry
            sl = pl.ds(pl.multiple_of(j * TOPK_BLK, TOPK_BLK), TOPK_BLK)
            hit = pos_ref[sl, e:e + 1] == slot
            acc_i = acc_i + jnp.sum(jnp.where(hit, tok + j * TOPK_BLK, 0), axis=0, keepdims=True)
            acc_g = acc_g + jnp.sum(jnp.where(hit, aff_ref[sl, e:e + 1], 0.0), axis=0, keepdims=True)
            return acc_i, acc_g
        acc_i, acc_g = lax.fori_loop(0, nblk, blk,
                                     (jnp.zeros((1, cap), I32), jnp.zeros((1, cap), F32)))
        idx_ref[e:e + 1, :] = acc_i
        gate_ref[e:e + 1, :] = acc_g


def _topk(aff, cap):
    n_tok = aff.shape[0]
    r = np.arange(TOPK_BLK)
    tri = jnp.asarray((r[None, :] <= r[:, None]).astype(np.float32), dtype=BF16)
    return pl.pallas_call(
        functools.partial(_topk_kernel, n_tok=n_tok, cap=cap),
        out_shape=[jax.ShapeDtypeStruct((N_EXPERTS, cap), I32),
                   jax.ShapeDtypeStruct((N_EXPERTS, cap), F32)],
        scratch_shapes=[pltpu.VMEM((n_tok, LANES), F32)],
        compiler_params=pltpu.CompilerParams(vmem_limit_bytes=VMEM_LIMIT), name="expert_topk",
    )(aff, tri)


ROW_DMA_UNROLL = 8


def _row_copies_start(hbm, vmem, idx_ref, e, cap, sem, gather):
    def issue(blk, carry):
        for u in range(ROW_DMA_UNROLL):
            p = blk * ROW_DMA_UNROLL + u
            t = idx_ref[e * cap + p]
            if gather:
                pltpu.make_async_copy(hbm.at[pl.ds(t, 1), :], vmem.at[pl.ds(p, 1), :], sem).start()
            else:
                pltpu.make_async_copy(vmem.at[pl.ds(p, 1), :], hbm.at[pl.ds(t, 1), :], sem).start()
        return carry
    lax.fori_loop(0, cap // ROW_DMA_UNROLL, issue, 0)


def _row_copies_wait(hbm, vmem, cap, sem, gather):
    if gather:
        pltpu.make_async_copy(hbm.at[pl.ds(0, cap), :], vmem, sem).wait()
    else:
        pltpu.make_async_copy(vmem, hbm.at[pl.ds(0, cap), :], sem).wait()


def _moe_kernel(idx_ref, gate_ref, fg_ref, x_hbm, acc_in_hbm, wg_ref, wu_ref, wd_ref, out_hbm,
                rows, xe, ye, sem, *, cap, nf):
    del acc_in_hbm
    e = pl.program_id(0)
    f = pl.program_id(1)

    @pl.when(f == 0)
    def _():
        _row_copies_start(x_hbm, rows, idx_ref, e, cap, sem, True)
        _row_copies_wait(x_hbm, rows, cap, sem, True)
        xe[...] = _rms(rows[...], fg_ref[...]).astype(BF16)
        ye[...] = jnp.zeros(ye.shape, F32)

    xb = xe[...]
    hg = _dot(xb, wg_ref[0].astype(BF16))
    hu = _dot(xb, wu_ref[0].astype(BF16))
    hid = hg * (1.0 / (1.0 + jnp.exp(-hg))) * hu
    ye[...] += _dot(hid.astype(BF16), wd_ref[0].astype(BF16))

    @pl.when(f == nf - 1)
    def _():
        _row_copies_start(out_hbm, rows, idx_ref, e, cap, sem, True)
        _row_copies_wait(out_hbm, rows, cap, sem, True)
        rows[...] = rows[...] + ye[...] * gate_ref[0]
        _row_copies_start(out_hbm, rows, idx_ref, e, cap, sem, False)
        _row_copies_wait(out_hbm, rows, cap, sem, False)


def _moe(x1, idx, gates, norm_ffn_g, w_gate, w_up, w_down, tf):
    n_tok = x1.shape[0]
    cap = idx.shape[1]
    nf = D_EXPERT // tf
    grid_spec = pltpu.PrefetchScalarGridSpec(
        num_scalar_prefetch=1,
        grid=(N_EXPERTS, nf),
        in_specs=[
            pl.BlockSpec((1, cap, 1), lambda e, f, s: (e, 0, 0)),
            pl.BlockSpec((1, D_MODEL), lambda e, f, s: (0, 0)),
            pl.BlockSpec(memory_space=pl.ANY),
            pl.BlockSpec(memory_space=pl.ANY),
            pl.BlockSpec((1, D_MODEL, tf), lambda e, f, s: (e, 0, f)),
            pl.BlockSpec((1, D_MODEL, tf), lambda e, f, s: (e, 0, f)),
            pl.BlockSpec((1, tf, D_MODEL), lambda e, f, s: (e, f, 0)),
        ],
        out_specs=pl.BlockSpec(memory_space=pl.ANY),
        scratch_shapes=[pltpu.VMEM((cap, D_MODEL), F32), pltpu.VMEM((cap, D_MODEL), BF16),
                        pltpu.VMEM((cap, D_MODEL), F32), pltpu.SemaphoreType.DMA(())],
    )
    return pl.pallas_call(
        functools.partial(_moe_kernel, cap=cap, nf=nf),
        grid_spec=grid_spec,
        out_shape=jax.ShapeDtypeStruct((n_tok, D_MODEL), F32),
        input_output_aliases={4: 0},
        compiler_params=_cparams(("arbitrary", "arbitrary")), name="expert_ffn",
    )(idx.reshape(N_EXPERTS * cap), gates.reshape(N_EXPERTS, cap, 1),
      norm_ffn_g.astype(F32).reshape(1, D_MODEL),
      x1, x1, w_gate, w_up, w_down)


def _ple_kernel(x_ref, p_ref, g_ref, wg_ref, wp_ref, o_ref):
    x = x_ref[...]
    zg = _dot(_rms(x, g_ref[...]).astype(BF16), wg_ref[...])
    gate = 1.0 / (1.0 + jnp.exp(-zg))
    o_ref[...] = x + gate * _dot(p_ref[...].astype(BF16), wp_ref[...])


def _ple(x2, p2d, norm_ple_g, w_ple_gate, w_ple_proj, tm):
    n_tok = x2.shape[0]
    row = lambda i: (i, 0)
    fixed = lambda i: (0, 0)
    return pl.pallas_call(
        _ple_kernel,
        grid=(n_tok // tm,),
        in_specs=[pl.BlockSpec((tm, D_MODEL), row), pl.BlockSpec((tm, PLE_DIM), row),
                  pl.BlockSpec((1, D_MODEL), fixed), pl.BlockSpec((D_MODEL, D_MODEL), fixed),
                  pl.BlockSpec((PLE_DIM, D_MODEL), fixed)],
        out_specs=pl.BlockSpec((tm, D_MODEL), row),
        out_shape=jax.ShapeDtypeStruct((n_tok, D_MODEL), F32),
        compiler_params=_cparams(("parallel",)), name="ple_gate",
    )(x2, p2d, norm_ple_g.astype(F32).reshape(1, D_MODEL), w_ple_gate.astype(BF16),
      w_ple_proj.astype(BF16))


def _tiles(seq):
    t_attn = min(512, seq)
    t_rec = min(512, seq)
    t_row = min(256, seq)
    return t_attn, t_rec, t_row


def _layer(x, p, norm_mix_g, w_in, q_norm_g, k_norm_g, diff_lambda, diff_norm_g, rec_lower_bound,
           rec_norm_g, w_out, norm_ffn_g, w_router, w_expert_gate, w_expert_up, w_expert_down,
           norm_ple_g, w_ple_gate, w_ple_proj):
    batch, seq, _ = x.shape
    n_tok = batch * seq
    t_attn, t_rec, t_row = _tiles(seq)
    x2d = x.reshape(n_tok, D_MODEL)
    qT, k, vT, rec = _projections(x2d, norm_mix_g[0], w_in[0], q_norm_g[0], k_norm_g[0], t_attn)
    ao = _attention(qT, k, vT, diff_lambda[0], diff_norm_g[0], batch, seq, t_attn)
    o_f = _recurrence(rec, rec_lower_bound, batch, seq, t_rec, False)
    o_b = _recurrence(rec, rec_lower_bound, batch, seq, t_rec, True)
    x1, aff = _outproj(x2d, ao, o_f, o_b, rec, rec_norm_g[0], w_out[0], norm_ffn_g[0],
                       w_router[0], t_row)
    cap = max(1, (CAPACITY_FACTOR * n_tok) // N_EXPERTS)
    idx, gates = _topk(aff, cap)
    x2 = _moe(x1, idx, gates, norm_ffn_g[0], w_expert_gate[0], w_expert_up[0], w_expert_down[0],
              tf=256)
    y = _ple(x2, p[0].reshape(n_tok, PLE_DIM), norm_ple_g[0], w_ple_gate[0], w_ple_proj[0], t_row)
    return y.reshape(x.shape)


def kernel(x_prompt, x_sample, p_prompt, p_sample, norm_mix_g, w_in, q_norm_g, k_norm_g, diff_lambda, diff_norm_g, rec_lower_bound, rec_norm_g, w_out, norm_ffn_g, w_router, w_expert_gate, w_expert_up, w_expert_down, norm_ple_g, w_ple_gate, w_ple_proj):
    weights = (norm_mix_g, w_in, q_norm_g, k_norm_g, diff_lambda, diff_norm_g, rec_lower_bound,
               rec_norm_g, w_out, norm_ffn_g, w_router, w_expert_gate, w_expert_up, w_expert_down,
               norm_ple_g, w_ple_gate, w_ple_proj)
    return (_layer(x_prompt, p_prompt, *weights), _layer(x_sample, p_sample, *weights))
```

```python
import functools
import math

import numpy as np
import jax
import jax.numpy as jnp
from jax import lax
from jax.experimental import pallas as pl
from jax.experimental.pallas import tpu as pltpu

F32 = jnp.float32
BF16 = jnp.bfloat16
I32 = jnp.int32

D_MODEL = 2048
PLE_DIM = 256
HEADS = 8
QK_DIM = 64
V_DIM = 128
HEAD_W = 128
ATTN_W = HEADS * V_DIM
REC_W = HEADS * V_DIM
N_EXPERTS = 16
CAPACITY_FACTOR = 2
D_EXPERT = 2048
NORM_EPS = 1e-6
LAM_INIT = 0.8 - 0.6 * math.exp(-0.3 * 0)

LANES = 128
LOG2E = 1.4426950408889634
VMEM_LIMIT = 56 * 1024 * 1024

REC_CHUNK = 64
REC_LEVELS = (1, 2, 4, 8, 16, 32)


def _cparams(sem, vmem=VMEM_LIMIT):
    return pltpu.CompilerParams(dimension_semantics=sem, vmem_limit_bytes=vmem)


def _dot(a, b):
    return jnp.dot(a, b, preferred_element_type=F32)


def _dot_nt(a, b):
    return lax.dot_general(a, b, (((1,), (1,)), ((), ())), preferred_element_type=F32)


def _dot_tn(a, b):
    return lax.dot_general(a, b, (((0,), (0,)), ((), ())), preferred_element_type=F32)


def _rms(x, g):
    ms = jnp.mean(x * x, axis=-1, keepdims=True)
    return x * lax.rsqrt(ms + NORM_EPS) * g


def _proj_prologue(x_ref, g_ref, h_ref):
    @pl.when(pl.program_id(1) == 0)
    def _():
        h_ref[...] = _rms(x_ref[...], g_ref[...]).astype(BF16)


def _proj_qT_kernel(x_ref, g_ref, wT_ref, qg_ref, o_ref, h_ref):
    _proj_prologue(x_ref, g_ref, h_ref)
    acc = _dot_nt(wT_ref[...], h_ref[...])
    tn, tm = acc.shape
    a3 = acc.reshape(tn // QK_DIM, QK_DIM, tm)
    ms = jnp.mean(a3 * a3, axis=1, keepdims=True)
    qn = (a3 * lax.rsqrt(ms + NORM_EPS)).reshape(tn, tm) * qg_ref[...]
    o_ref[0] = qn.astype(o_ref.dtype)


def _proj_k_kernel(x_ref, g_ref, w_ref, kg_ref, bd_ref, o_ref, h_ref):
    _proj_prologue(x_ref, g_ref, h_ref)
    acc = _dot(h_ref[...], w_ref[...])
    ss = _dot((acc * acc).astype(BF16), bd_ref[...])
    kn = acc * lax.rsqrt(ss * (1.0 / QK_DIM) + NORM_EPS) * kg_ref[...]
    o_ref[...] = kn.astype(o_ref.dtype)


def _proj_vT_kernel(x_ref, g_ref, wT_ref, o_ref, h_ref):
    _proj_prologue(x_ref, g_ref, h_ref)
    o_ref[0] = _dot_nt(wT_ref[...], h_ref[...]).astype(o_ref.dtype)


def _proj_rec_kernel(x_ref, g_ref, w_ref, o_ref, h_ref, *, blocks_per_seg):
    _proj_prologue(x_ref, g_ref, h_ref)
    acc = _dot(h_ref[...], w_ref[...])
    seg = pl.program_id(1) // blocks_per_seg
    is_silu = jnp.logical_or(seg == 0, seg == 4)

    @pl.when(is_silu)
    def _():
        o_ref[...] = acc * (1.0 / (1.0 + jnp.exp(-acc)))

    @pl.when(jnp.logical_not(is_silu))
    def _():
        o_ref[...] = acc


def _projections(x2d, norm_g, w_in, q_norm_g, k_norm_g, tm):
    n_tok = x2d.shape[0]
    nb = n_tok // tm
    g_row = norm_g.reshape(1, D_MODEL)
    w_bf = w_in.astype(BF16)
    x_spec = pl.BlockSpec((tm, D_MODEL), lambda i, j: (i, 0))
    g_spec = pl.BlockSpec((1, D_MODEL), lambda i, j: (0, 0))
    scratch = [pltpu.VMEM((tm, D_MODEL), BF16)]
    sem = ("parallel", "arbitrary")

    tn = 512
    wqT = w_bf[:, 0:ATTN_W].T
    qg_col = jnp.tile(q_norm_g.astype(F32) * (QK_DIM ** -0.5 * LOG2E), 2 * HEADS).reshape(ATTN_W, 1)
    qT = pl.pallas_call(
        _proj_qT_kernel,
        grid=(nb, ATTN_W // tn),
        in_specs=[x_spec, g_spec,
                  pl.BlockSpec((tn, D_MODEL), lambda i, j: (j, 0)),
                  pl.BlockSpec((tn, 1), lambda i, j: (j, 0))],
        out_specs=pl.BlockSpec((1, tn, tm), lambda i, j: (i, j, 0)),
        out_shape=jax.ShapeDtypeStruct((nb, ATTN_W, tm), BF16),
        scratch_shapes=scratch, compiler_params=_cparams(sem), name="proj_qT",
    )(x2d, g_row, wqT, qg_col)

    tnk = 256
    kg_row = jnp.tile(k_norm_g.astype(F32), 2 * HEADS).reshape(1, ATTN_W)
    grp = np.arange(tnk) // QK_DIM
    bd = jnp.asarray((grp[:, None] == grp[None, :]).astype(np.float32), dtype=BF16)
    k = pl.pallas_call(
        _proj_k_kernel,
        grid=(nb, ATTN_W // tnk),
        in_specs=[x_spec, g_spec,
                  pl.BlockSpec((D_MODEL, tnk), lambda i, j: (0, j)),
                  pl.BlockSpec((1, tnk), lambda i, j: (0, j)),
                  pl.BlockSpec((tnk, tnk), lambda i, j: (0, 0))],
        out_specs=pl.BlockSpec((tm, tnk), lambda i, j: (i, j)),
        out_shape=jax.ShapeDtypeStruct((n_tok, ATTN_W), BF16),
        scratch_shapes=scratch, compiler_params=_cparams(sem), name="proj_k",
    )(x2d, g_row, w_bf[:, ATTN_W:2 * ATTN_W], kg_row, bd)

    wvT = w_bf[:, 2 * ATTN_W:3 * ATTN_W].T
    vT = pl.pallas_call(
        _proj_vT_kernel,
        grid=(nb, ATTN_W // tn),
        in_specs=[x_spec, g_spec, pl.BlockSpec((tn, D_MODEL), lambda i, j: (j, 0))],
        out_specs=pl.BlockSpec((1, tn, tm), lambda i, j: (i, j, 0)),
        out_shape=jax.ShapeDtypeStruct((nb, ATTN_W, tm), BF16),
        scratch_shapes=scratch, compiler_params=_cparams(sem), name="proj_vT",
    )(x2d, g_row, wvT)

    rec_w = 5 * REC_W
    tnr = REC_W
    rec = pl.pallas_call(
        functools.partial(_proj_rec_kernel, blocks_per_seg=REC_W // tnr),
        grid=(nb, rec_w // tnr),
        in_specs=[x_spec, g_spec, pl.BlockSpec((D_MODEL, tnr), lambda i, j: (0, j))],
        out_specs=pl.BlockSpec((tm, tnr), lambda i, j: (i, j)),
        out_shape=jax.ShapeDtypeStruct((n_tok, rec_w), F32),
        scratch_shapes=scratch, compiler_params=_cparams(sem), name="proj_rec",
    )(x2d, g_row, w_bf[:, 3 * ATTN_W:])
    return qT, k, vT, rec


L_ROWS = 16
SKIP_LOG2_MARGIN = 160.0
SKIP_NORM_SLACK = 1.02


def _attn_kernel(slopes_ref, qT_ref, k_ref, vT_ref, lam_ref, g_ref, o_ref,
                 m_ref, acc_ref, bias_ref, ta_ref, tb_ref, mta_ref, mtb_ref, kn_ref, *, nk, tk, tq):
    h = pl.program_id(1)
    i = pl.program_id(2)
    sl = slopes_ref[h]
    lane_h = lax.broadcasted_iota(I32, (1, HEAD_W), 1)
    lane_v = lax.broadcasted_iota(I32, (1, LANES), 1)

    @pl.when(i == 0)
    def _():
        rel = (lax.broadcasted_iota(I32, (tk, tq), 1)
               - lax.broadcasted_iota(I32, (tk, tq), 0)).astype(F32)
        sr = sl * rel
        sa = sl * jnp.abs(rel)
        bias_ref[0] = jnp.concatenate([sr, sr], axis=1)
        bias_ref[1] = jnp.concatenate([sa, sa], axis=1)
        bias_ref[2] = jnp.concatenate([-sr, -sr], axis=1)

        def block_norm(j, carry):
            kf = k_ref[pl.ds(pl.multiple_of(j * tk, tk), tk), :].astype(F32)
            sq = kf * kf
            out = []
            for c in range(2):
                in_map = (lane_h >= c * QK_DIM) & (lane_h < (c + 1) * QK_DIM)
                n2 = jnp.sum(jnp.where(in_map, sq, 0.0), axis=1, keepdims=True)
                nmax = jnp.sqrt(jnp.max(n2, axis=0, keepdims=True))
                out.append(jnp.where(lane_v == j, nmax, carry[c]))
            return tuple(out)

        kn0, kn1 = lax.fori_loop(0, nk, block_norm,
                                 (jnp.zeros((1, LANES), F32), jnp.zeros((1, LANES), F32)))
        kn_ref[0:1, :] = kn0
        kn_ref[1:2, :] = kn1

    qT = qT_ref[0]

    qf = qT.astype(F32)
    qsq = qf * qf
    thr = None
    for c in range(2):
        rows_c = slice(c * QK_DIM, (c + 1) * QK_DIM)
        bq = jnp.sqrt(jnp.max(jnp.sum(qsq[rows_c], axis=0, keepdims=True), axis=1, keepdims=True))
        kn = kn_ref[c:c + 1, :]
        k_all = jnp.max(kn, axis=1, keepdims=True)
        k_diag = jnp.max(jnp.where(lane_v == i, kn, 0.0), axis=1, keepdims=True)
        thr_c = SKIP_NORM_SLACK * bq * (k_all + k_diag) + SKIP_LOG2_MARGIN
        thr = thr_c if thr is None else jnp.maximum(thr, thr_c)
    dist = lane_v.astype(F32)
    visit = (lane_v >= 1) & (sl * ((dist - 1.0) * tq + 1.0) <= thr)
    radius = jnp.sum(visit.astype(I32))
    lo = jnp.maximum(i - radius, 0)
    hi = jnp.minimum(i + radius, nk - 1)
    n_vis = hi - lo + 1
    zero_half = jnp.zeros((QK_DIM, tq), qT.dtype)
    q_cat = jnp.concatenate([jnp.concatenate([qT[:QK_DIM], zero_half], axis=0),
                             jnp.concatenate([zero_half, qT[QK_DIM:]], axis=0)], axis=1)
    ones = jnp.ones((L_ROWS, tk), BF16)
    m_ref[...] = jnp.full(m_ref.shape, -jnp.inf, F32)
    acc_ref[...] = jnp.zeros(acc_ref.shape, F32)

    def shift_of(j):
        gap = jnp.abs(jnp.full((1, 2 * tq), (i - j) * tq, I32)).astype(F32)
        return -sl * gap

    def scores(j, t_ref, mt_ref):
        kblk = k_ref[pl.ds(pl.multiple_of(j * tk, tk), tk), :]
        side = jnp.where(j < i, 0, jnp.where(j == i, 1, 2))
        t = _dot(kblk, q_cat) - bias_ref[side]
        t_ref[...] = t
        mt_ref[...] = jnp.max(t, axis=0, keepdims=True) + shift_of(j)

    def accumulate(j, t_ref, mt_ref):
        m_old = m_ref[...]
        m_new = jnp.maximum(m_old, mt_ref[...])
        p = jnp.exp2(t_ref[...] - (m_new - shift_of(j)))
        v_aug = jnp.concatenate([vT_ref[j], ones], axis=0)
        acc_ref[...] = jnp.exp2(m_old - m_new) * acc_ref[...] + _dot(v_aug, p.astype(BF16))
        m_ref[...] = m_new

    def pair(jj, carry):
        j = lo + 2 * jj
        scores(j + 1, tb_ref, mtb_ref)
        accumulate(j, ta_ref, mta_ref)
        scores(jnp.minimum(j + 2, hi), ta_ref, mta_ref)
        accumulate(j + 1, tb_ref, mtb_ref)
        return carry

    scores(lo, ta_ref, mta_ref)
    lax.fori_loop(0, jnp.right_shift(n_vis, 1), pair, 0)

    @pl.when(jnp.bitwise_and(n_vis, 1) == 1)
    def _():
        accumulate(hi, ta_ref, mta_ref)

    lp = lam_ref[...]
    lam = (jnp.exp(jnp.sum(lp[0:1] * lp[1:2], axis=-1, keepdims=True))
           - jnp.exp(jnp.sum(lp[2:3] * lp[3:4], axis=-1, keepdims=True)) + LAM_INIT)
    acc = acc_ref[...]
    a0 = acc[:, :tq]
    a1 = acc[:, tq:]
    o = (a0[:V_DIM] / a0[V_DIM:V_DIM + 1]
         - lam * (a1[:V_DIM] / a1[V_DIM:V_DIM + 1]))
    ms = jnp.mean(o * o, axis=0, keepdims=True)
    on = o * lax.rsqrt(ms + NORM_EPS) * g_ref[...] * (1.0 - LAM_INIT)
    o_ref[...] = on.T.astype(o_ref.dtype)


def _attention(qT, k, vT, diff_lambda, diff_norm_g, batch, seq, tq):
    nq = seq // tq
    assert nq <= LANES, "per-block key norms are kept one per lane"
    slopes = LOG2E * jnp.exp2(-8.0 * jnp.arange(1, HEADS + 1, dtype=F32) / HEADS)
    g_col = diff_norm_g.astype(F32).reshape(ATTN_W, 1)
    grid_spec = pltpu.PrefetchScalarGridSpec(
        num_scalar_prefetch=1,
        grid=(batch, HEADS, nq),
        in_specs=[
            pl.BlockSpec((1, HEAD_W, tq), lambda b, h, i, s: (b * nq + i, h, 0)),
            pl.BlockSpec((seq, HEAD_W), lambda b, h, i, s: (b, h)),
            pl.BlockSpec((nq, HEAD_W, tq), lambda b, h, i, s: (b, h, 0)),
            pl.BlockSpec((4, QK_DIM), lambda b, h, i, s: (0, 0)),
            pl.BlockSpec((HEAD_W, 1), lambda b, h, i, s: (h, 0)),
        ],
        out_specs=pl.BlockSpec((tq, HEAD_W), lambda b, h, i, s: (b * nq + i, h)),
        scratch_shapes=[pltpu.VMEM((1, 2 * tq), F32),
                        pltpu.VMEM((V_DIM + L_ROWS, 2 * tq), F32),
                        pltpu.VMEM((3, tq, 2 * tq), F32),
                        pltpu.VMEM((tq, 2 * tq), F32), pltpu.VMEM((tq, 2 * tq), F32),
                        pltpu.VMEM((1, 2 * tq), F32), pltpu.VMEM((1, 2 * tq), F32),
                        pltpu.VMEM((2, LANES), F32)],
    )
    return pl.pallas_call(
        functools.partial(_attn_kernel, nk=nq, tk=tq, tq=tq),
        grid_spec=grid_spec,
        out_shape=jax.ShapeDtypeStruct((batch * seq, ATTN_W), BF16),
        compiler_params=_cparams(("arbitrary", "arbitrary", "arbitrary")), name="diff_attn",
    )(slopes, qT, k, vT, diff_lambda.astype(F32), g_col)


def _rec_mats(reverse):
    c = REC_CHUNK
    t = np.arange(c)[:, None]
    u = np.arange(c)[None, :]
    mats = [(u >= t) if reverse else (u <= t)]
    for lv in REC_LEVELS:
        if reverse:
            bd = (t // (2 * lv)) * (2 * lv) + lv
            mats.append(u >= bd)
        else:
            bd = (t // (2 * lv)) * (2 * lv) + lv - 1
            mats.append(u <= bd)
    return jnp.asarray(np.concatenate(mats, axis=0).astype(np.float32), dtype=BF16)


def _rec_kernel(q_ref, z_ref, v_ref, lbp_ref, mats_ref, o_ref, st_ref, *, reverse, n_chunks):
    c = REC_CHUNK

    @pl.when(pl.program_id(2) == 0)
    def _():
        st_ref[...] = jnp.zeros(st_ref.shape, F32)

    lbp = lbp_ref[...]
    mx = jnp.maximum(lbp[0:1], lbp[1:2])
    e0 = jnp.exp(lbp[0:1] - mx)
    e1 = jnp.exp(lbp[1:2] - mx)
    lb = e0 / (e0 + e1)

    rows = [slice(n * c, (n + 1) * c) for n in range(n_chunks)]
    qs = [q_ref[sl, :] for sl in rows]
    vs = [v_ref[sl, :].astype(BF16) for sl in rows]
    gs, kks = [], []
    for sl in rows:
        z = z_ref[sl, :]
        sig = 1.0 / (1.0 + jnp.exp(-z))
        gs.append(jnp.log(lb + (1.0 - lb) * sig))
        kks.append((1.0 - lb) * (1.0 / (1.0 + jnp.exp(z))))

    g_cat = jnp.concatenate(gs, axis=1)
    g1 = g_cat.astype(BF16)
    r1 = g_cat - g1.astype(F32)
    g2 = r1.astype(BF16)
    g3 = (r1 - g2.astype(F32)).astype(BF16)
    mats = mats_ref[...]
    stacked = (_dot(mats, g3) + _dot(mats, g2)) + _dot(mats, g1)

    row = lax.broadcasted_iota(I32, (c, HEAD_W), 0)
    ti = lax.broadcasted_iota(I32, (c, c), 0)
    si = lax.broadcasted_iota(I32, (c, c), 1)
    is_query, same = [], []
    for lv in REC_LEVELS:
        in_upper = (row & (2 * lv - 1)) >= lv
        is_query.append(jnp.logical_not(in_upper) if reverse else in_upper)
        shift = (2 * lv).bit_length() - 1
        same.append((ti >> shift) == (si >> shift))
    diag = ti == si

    bs, o_intra, incs, qes, decays = [], [], [], [], []
    for n in range(n_chunks):
        lanes = slice(n * HEAD_W, (n + 1) * HEAD_W)
        b = stacked[0:c, lanes]
        q, kk = qs[n], kks[n]
        a = jnp.where(diag, _dot_nt(q.astype(BF16), kk.astype(BF16)), 0.0)
        for m, lv in enumerate(REC_LEVELS):
            r = stacked[(m + 1) * c:(m + 2) * c, lanes]
            e = jnp.exp(-jnp.abs(b - r))
            qt = jnp.where(is_query[m], q * e, 0.0).astype(BF16)
            kt = jnp.where(is_query[m], 0.0, kk * e).astype(BF16)
            a = a + jnp.where(same[m], _dot_nt(qt, kt), 0.0)
        b_end = b[0:1] if reverse else b[c - 1:c]
        o_intra.append(_dot(a.astype(BF16), vs[n]))
        incs.append(_dot_tn(vs[n], (kk * jnp.exp(b_end - b)).astype(BF16)))
        qes.append((q * jnp.exp(b)).astype(BF16))
        decays.append(jnp.exp(b_end))

    st = st_ref[...]
    order = range(n_chunks - 1, -1, -1) if reverse else range(n_chunks)
    for n in order:
        o_ref[rows[n], :] = o_intra[n] + _dot_nt(qes[n], st.astype(BF16))
        st = st * decays[n] + incs[n]
    st_ref[...] = st


def _recurrence(rec, rec_lower_bound, batch, seq, tb, reverse):
    nb = seq // tb
    d = 1 if reverse else 0
    mats = _rec_mats(reverse)
    lbp = rec_lower_bound[d].astype(F32)

    def blk(b, i):
        return b * nb + (nb - 1 - i if reverse else i)

    return pl.pallas_call(
        functools.partial(_rec_kernel, reverse=reverse, n_chunks=tb // REC_CHUNK),
        grid=(batch, HEADS, nb),
        in_specs=[
            pl.BlockSpec((tb, HEAD_W), lambda b, h, i: (blk(b, i), h)),
            pl.BlockSpec((tb, HEAD_W), lambda b, h, i: (blk(b, i), HEADS * (1 + d) + h)),
            pl.BlockSpec((tb, HEAD_W), lambda b, h, i: (blk(b, i), HEADS * 3 + h)),
            pl.BlockSpec((2, HEAD_W), lambda b, h, i: (0, h)),
            pl.BlockSpec(mats.shape, lambda b, h, i: (0, 0)),
        ],
        out_specs=pl.BlockSpec((tb, HEAD_W), lambda b, h, i: (blk(b, i), h)),
        out_shape=jax.ShapeDtypeStruct((batch * seq, REC_W), F32),
        scratch_shapes=[pltpu.VMEM((HEAD_W, HEAD_W), F32)],
        compiler_params=_cparams(("parallel", "parallel", "arbitrary")),
        name="rec_bwd" if reverse else "rec_fwd",
    )(rec, rec, rec, lbp, mats)


def _split_bf16(x):
    hi = x.astype(BF16)
    lo = (x - hi.astype(F32)).astype(BF16)
    return hi, lo


def _outproj_kernel(x_ref, ao_ref, of_ref, ob_ref, rg_ref, rng_ref, woa_ref, wor_ref,
                    fg_ref, wrh_ref, wrl_ref, x1_ref, aff_ref):
    parts = []
    for hd in range(HEADS):
        sl = slice(hd * HEAD_W, (hd + 1) * HEAD_W)
        o = of_ref[:, sl] + ob_ref[:, sl]
        parts.append((_rms(o, rng_ref[:, sl]) * rg_ref[:, sl]).astype(BF16))
    ro = jnp.concatenate(parts, axis=-1)
    x1 = x_ref[...] + _dot(ao_ref[...], woa_ref[...]) + _dot(ro, wor_ref[...])
    x1_ref[...] = x1
    h2 = _rms(x1, fg_ref[...])
    hi, lo = _split_bf16(h2)
    logits = _dot(hi, wrh_ref[...]) + _dot(lo, wrh_ref[...]) + _dot(hi, wrl_ref[...])
    lane = lax.broadcasted_iota(I32, logits.shape, 1)
    logits = jnp.where(lane < N_EXPERTS, logits, -jnp.inf)
    mx = jnp.max(logits, axis=-1, keepdims=True)
    ex = jnp.exp(logits - mx)
    aff = ex / jnp.sum(ex, axis=-1, keepdims=True)
    aff_ref[...] = aff.T[:N_EXPERTS]


def _outproj(x2d, ao, o_f, o_b, rec, rec_norm_g, w_out, norm_ffn_g, w_router, tm):
    n_tok = x2d.shape[0]
    w_bf = w_out.astype(BF16)
    wr = jnp.pad(w_router.astype(F32), ((0, 0), (0, LANES - N_EXPERTS)))
    wr_hi = wr.astype(BF16)
    wr_lo = (wr - wr_hi.astype(F32)).astype(BF16)
    row = lambda i: (i, 0)
    fixed = lambda i: (0, 0)
    return pl.pallas_call(
        _outproj_kernel,
        grid=(n_tok // tm,),
        in_specs=[
            pl.BlockSpec((tm, D_MODEL), row),
            pl.BlockSpec((tm, ATTN_W), row),
            pl.BlockSpec((tm, REC_W), row),
            pl.BlockSpec((tm, REC_W), row),
            pl.BlockSpec((tm, REC_W), lambda i: (i, 4)),
            pl.BlockSpec((1, REC_W), fixed),
            pl.BlockSpec((ATTN_W, D_MODEL), fixed),
            pl.BlockSpec((REC_W, D_MODEL), fixed),
            pl.BlockSpec((1, D_MODEL), fixed),
            pl.BlockSpec((D_MODEL, LANES), fixed),
            pl.BlockSpec((D_MODEL, LANES), fixed),
        ],
        out_specs=[pl.BlockSpec((tm, D_MODEL), row), pl.BlockSpec((N_EXPERTS, tm), lambda i: (0, i))],
        out_shape=[jax.ShapeDtypeStruct((n_tok, D_MODEL), F32),
                   jax.ShapeDtypeStruct((N_EXPERTS, n_tok), F32)],
        compiler_params=_cparams(("parallel",)), name="outproj_router",
    )(x2d, ao, o_f, o_b, rec, rec_norm_g.astype(F32).reshape(1, REC_W),
      w_bf[:ATTN_W], w_bf[ATTN_W:], norm_ffn_g.astype(F32).reshape(1, D_MODEL), wr_hi, wr_lo)


TOPK_BLK = 256


def _topk_kernel(aff_ref, tri_ref, idx_ref, gate_ref, *, n_tok, cap):
    aff = aff_ref[...]

    def search(step, cand):
        trial = cand | (jnp.int32(1) << (30 - step))
        n_ge = jnp.sum((aff >= pltpu.bitcast(trial, F32)).astype(I32), axis=1, keepdims=True)
        return jnp.where(n_ge >= cap, trial, cand)

    thr = pltpu.bitcast(lax.fori_loop(0, 31, search, jnp.zeros((N_EXPERTS, 1), I32)), F32)
    gt = aff > thr
    tie = aff == thr
    need = (cap - jnp.sum(gt.astype(I32), axis=1, keepdims=True)).astype(F32)

    def running_count(flags):
        f = flags.astype(F32).astype(BF16)
        carry = jnp.zeros((N_EXPERTS, 1), F32)
        out = []
        for j in range(n_tok // TOPK_BLK):
            pre = _dot(f[:, j * TOPK_BLK:(j + 1) * TOPK_BLK], tri_ref[...]) + carry
            out.append(pre)
            carry = pre[:, TOPK_BLK - 1:TOPK_BLK]
        return jnp.concatenate(out, axis=1)

    sel = jnp.logical_or(gt, jnp.logical_and(tie, running_count(tie) <= need))
    slot = running_count(sel).astype(I32) - 1

    lane = lax.broadcasted_iota(I32, (N_EXPERTS, n_tok), 1)
    packed = jnp.where(sel, (slot << 16) | lane, -1)
    val = aff
    for k in range((n_tok - 1).bit_length()):
        step = 1 << k
        dist = lane - (packed >> 16)
        moving = jnp.logical_and(packed >= 0, ((dist >> k) & 1) == 1)
        arriving = pltpu.roll(jnp.where(moving, packed, -1), n_tok - step, axis=1)
        arriving_val = pltpu.roll(val, n_tok - step, axis=1)
        lands = arriving >= 0
        packed = jnp.where(lands, arriving, jnp.where(moving, -1, packed))
        val = jnp.where(lands, arriving_val, val)
    idx_ref[...] = packed[:, :cap] & 0xFFFF
    gate_ref[...] = val[:, :cap]


def _topk(aff_t, cap):
    n_tok = aff_t.shape[1]
    assert n_tok <= 32768 and n_tok % TOPK_BLK == 0, "token index and slot share one int32"
    r = np.arange(TOPK_BLK)
    tri = jnp.asarray((r[:, None] <= r[None, :]).astype(np.float32), dtype=BF16)
    return pl.pallas_call(
        functools.partial(_topk_kernel, n_tok=n_tok, cap=cap),
        out_shape=[jax.ShapeDtypeStruct((N_EXPERTS, cap), I32),
                   jax.ShapeDtypeStruct((N_EXPERTS, cap), F32)],
        compiler_params=pltpu.CompilerParams(vmem_limit_bytes=VMEM_LIMIT), name="expert_topk",
    )(aff_t, tri)


ROW_DMA_UNROLL = 8


def _row_copy(hbm, vmem, t, p, sem, gather):
    if gather:
        return pltpu.make_async_copy(hbm.at[pl.ds(t, 1), :], vmem.at[pl.ds(p, 1), :], sem)
    return pltpu.make_async_copy(vmem.at[pl.ds(p, 1), :], hbm.at[pl.ds(t, 1), :], sem)


def _rows_start_loop(hbm, vmem, idx_ref, base, cap, sem, gather):
    def issue(blk, carry):
        for u in range(ROW_DMA_UNROLL):
            p = blk * ROW_DMA_UNROLL + u
            _row_copy(hbm, vmem, idx_ref[base + p], p, sem, gather).start()
        return carry
    lax.fori_loop(0, cap // ROW_DMA_UNROLL, issue, 0)


def _rows_start_inline(hbm, vmem, idx_ref, base, p0, count, sem, gather):
    for u in range(count):
        _row_copy(hbm, vmem, idx_ref[base + p0 + u], p0 + u, sem, gather).start()


def _rows_wait(hbm, vmem, cap, sem, gather):
    if gather:
        pltpu.make_async_copy(hbm.at[pl.ds(0, cap), :], vmem, sem).wait()
    else:
        pltpu.make_async_copy(vmem, hbm.at[pl.ds(0, cap), :], sem).wait()


def _moe_kernel(idx_ref, gate_ref, fg_ref, x_hbm, acc_in_hbm, wg_ref, wu_ref, wd_ref, out_hbm,
                xg, rows, xe, ye, sem_x, sem_r, sem_s, *, cap, nf):
    del acc_in_hbm
    e = pl.program_id(0)
    f = pl.program_id(1)
    last_e = pl.num_programs(0) - 1
    chunk = cap // nf

    @pl.when(f == 0)
    def _():
        @pl.when(e == 0)
        def _():
            _rows_start_loop(x_hbm, xg, idx_ref, 0, cap, sem_x, True)
            _rows_wait(x_hbm, xg, cap, sem_x, True)

        @pl.when(e > 0)
        def _():
            _rows_wait(out_hbm, rows, cap, sem_s, False)

        xe[...] = _rms(xg[...], fg_ref[...]).astype(BF16)
        ye[...] = jnp.zeros(ye.shape, F32)

    _rows_start_inline(x_hbm, xg, idx_ref, jnp.minimum(e + 1, last_e) * cap, f * chunk, chunk,
                       sem_x, True)
    _rows_start_inline(out_hbm, rows, idx_ref, e * cap, f * chunk, chunk, sem_r, True)

    xb = xe[...]
    hg = _dot(xb, wg_ref[0].astype(BF16))
    hu = _dot(xb, wu_ref[0].astype(BF16))
    hid = hg * (1.0 / (1.0 + jnp.exp(-hg))) * hu
    ye[...] += _dot(hid.astype(BF16), wd_ref[0].astype(BF16))

    @pl.when(f == nf - 1)
    def _():
        _rows_wait(x_hbm, xg, cap, sem_x, True)
        _rows_wait(out_hbm, rows, cap, sem_r, True)
        rows[...] = rows[...] + ye[...] * gate_ref[0]
        _rows_start_loop(out_hbm, rows, idx_ref, e * cap, cap, sem_s, False)

        @pl.when(e == last_e)
        def _():
            _rows_wait(out_hbm, rows, cap, sem_s, False)


def _moe(x1, idx, gates, norm_ffn_g, w_gate, w_up, w_down, tf):
    n_tok = x1.shape[0]
    cap = idx.shape[1]
    nf = D_EXPERT // tf
    grid_spec = pltpu.PrefetchScalarGridSpec(
        num_scalar_prefetch=1,
        grid=(N_EXPERTS, nf),
        in_specs=[
            pl.BlockSpec((1, cap, 1), lambda e, f, s: (e, 0, 0)),
            pl.BlockSpec((1, D_MODEL), lambda e, f, s: (0, 0)),
            pl.BlockSpec(memory_space=pl.ANY),
            pl.BlockSpec(memory_space=pl.ANY),
            pl.BlockSpec((1, D_MODEL, tf), lambda e, f, s: (e, 0, f)),
            pl.BlockSpec((1, D_MODEL, tf), lambda e, f, s: (e, 0, f)),
            pl.BlockSpec((1, tf, D_MODEL), lambda e, f, s: (e, f, 0)),
        ],
        out_specs=pl.BlockSpec(memory_space=pl.ANY),
        scratch_shapes=[pltpu.VMEM((cap, D_MODEL), F32), pltpu.VMEM((cap, D_MODEL), F32),
                        pltpu.VMEM((cap, D_MODEL), BF16), pltpu.VMEM((cap, D_MODEL), F32),
                        pltpu.SemaphoreType.DMA(()), pltpu.SemaphoreType.DMA(()),
                        pltpu.SemaphoreType.DMA(())],
    )
    return pl.pallas_call(
        functools.partial(_moe_kernel, cap=cap, nf=nf),
        grid_spec=grid_spec,
        out_shape=jax.ShapeDtypeStruct((n_tok, D_MODEL), F32),
        input_output_aliases={4: 0},
        compiler_params=_cparams(("arbitrary", "arbitrary")), name="expert_ffn",
    )(idx.reshape(N_EXPERTS * cap), gates.reshape(N_EXPERTS, cap, 1),
      norm_ffn_g.astype(F32).reshape(1, D_MODEL),
      x1, x1, w_gate, w_up, w_down)


def _ple_kernel(x_ref, p_ref, g_ref, wg_ref, wp_ref, o_ref):
    x = x_ref[...]
    zg = _dot(_rms(x, g_ref[...]).astype(BF16), wg_ref[...])
    gate = 1.0 / (1.0 + jnp.exp(-zg))
    o_ref[...] = x + gate * _dot(p_ref[...].astype(BF16), wp_ref[...])


def _ple(x2, p2d, norm_ple_g, w_ple_gate, w_ple_proj, tm):
    n_tok = x2.shape[0]
    row = lambda i: (i, 0)
    fixed = lambda i: (0, 0)
    return pl.pallas_call(
        _ple_kernel,
        grid=(n_tok // tm,),
        in_specs=[pl.BlockSpec((tm, D_MODEL), row), pl.BlockSpec((tm, PLE_DIM), row),
                  pl.BlockSpec((1, D_MODEL), fixed), pl.BlockSpec((D_MODEL, D_MODEL), fixed),
                  pl.BlockSpec((PLE_DIM, D_MODEL), fixed)],
        out_specs=pl.BlockSpec((tm, D_MODEL), row),
        out_shape=jax.ShapeDtypeStruct((n_tok, D_MODEL), F32),
        compiler_params=_cparams(("parallel",)), name="ple_gate",
    )(x2, p2d, norm_ple_g.astype(F32).reshape(1, D_MODEL), w_ple_gate.astype(BF16),
      w_ple_proj.astype(BF16))


def _tiles(seq):
    t_attn = min(512, seq)
    t_rec = min(512, seq)
    t_row = min(256, seq)
    return t_attn, t_rec, t_row


def _layer(x, p, norm_mix_g, w_in, q_norm_g, k_norm_g, diff_lambda, diff_norm_g, rec_lower_bound,
           rec_norm_g, w_out, norm_ffn_g, w_router, w_expert_gate, w_expert_up, w_expert_down,
           norm_ple_g, w_ple_gate, w_ple_proj):
    batch, seq, _ = x.shape
    n_tok = batch * seq
    t_attn, t_rec, t_row = _tiles(seq)
    x2d = x.reshape(n_tok, D_MODEL)
    qT, k, vT, rec = _projections(x2d, norm_mix_g[0], w_in[0], q_norm_g[0], k_norm_g[0], t_attn)
    ao = _attention(qT, k, vT, diff_lambda[0], diff_norm_g[0], batch, seq, t_attn)
    o_f = _recurrence(rec, rec_lower_bound, batch, seq, t_rec, False)
    o_b = _recurrence(rec, rec_lower_bound, batch, seq, t_rec, True)
    x1, aff = _outproj(x2d, ao, o_f, o_b, rec, rec_norm_g[0], w_out[0], norm_ffn_g[0],
                       w_router[0], t_row)
    cap = max(1, (CAPACITY_FACTOR * n_tok) // N_EXPERTS)
    idx, gates = _topk(aff, cap)
    x2 = _moe(x1, idx, gates, norm_ffn_g[0], w_expert_gate[0], w_expert_up[0], w_expert_down[0],
              tf=256)
    y = _ple(x2, p[0].reshape(n_tok, PLE_DIM), norm_ple_g[0], w_ple_gate[0], w_ple_proj[0], t_row)
    return y.reshape(x.shape)


def kernel(x_prompt, x_sample, p_prompt, p_sample, norm_mix_g, w_in, q_norm_g, k_norm_g, diff_lambda, diff_norm_g, rec_lower_bound, rec_norm_g, w_out, norm_ffn_g, w_router, w_expert_gate, w_expert_up, w_expert_down, norm_ple_g, w_ple_gate, w_ple_proj):
    weights = (norm_mix_g, w_in, q_norm_g, k_norm_g, diff_lambda, diff_norm_g, rec_lower_bound,
               rec_norm_g, w_out, norm_ffn_g, w_router, w_expert_gate, w_expert_up, w_expert_down,
               norm_ple_g, w_ple_gate, w_ple_proj)
    return (_layer(x_prompt, p_prompt, *weights), _layer(x_sample, p_sample, *weights))
```

```python
import functools
import math

import numpy as np
import jax
import jax.numpy as jnp
from jax import lax
from jax.experimental import pallas as pl
from jax.experimental.pallas import tpu as pltpu

F32 = jnp.float32
BF16 = jnp.bfloat16
I32 = jnp.int32

D_MODEL = 2048
PLE_DIM = 256
HEADS = 8
QK_DIM = 64
V_DIM = 128
HEAD_W = 128
ATTN_W = HEADS * V_DIM
REC_W = HEADS * V_DIM
N_EXPERTS = 16
CAPACITY_FACTOR = 2
D_EXPERT = 2048
NORM_EPS = 1e-6
LAM_INIT = 0.8 - 0.6 * math.exp(-0.3 * 0)

LANES = 128
LOG2E = 1.4426950408889634
VMEM_LIMIT = 56 * 1024 * 1024

REC_CHUNK = 64
REC_LEVELS = (1, 2, 4, 8, 16, 32)


def _cparams(sem, vmem=VMEM_LIMIT):
    return pltpu.CompilerParams(dimension_semantics=sem, vmem_limit_bytes=vmem)


def _dot(a, b):
    return jnp.dot(a, b, preferred_element_type=F32)


def _dot_nt(a, b):
    return lax.dot_general(a, b, (((1,), (1,)), ((), ())), preferred_element_type=F32)


def _dot_tn(a, b):
    return lax.dot_general(a, b, (((0,), (0,)), ((), ())), preferred_element_type=F32)


def _rms(x, g):
    ms = jnp.mean(x * x, axis=-1, keepdims=True)
    return x * lax.rsqrt(ms + NORM_EPS) * g


def _proj_prologue(x_ref, g_ref, h_ref):
    @pl.when(pl.program_id(1) == 0)
    def _():
        h_ref[...] = _rms(x_ref[...], g_ref[...]).astype(BF16)


def _proj_qT_kernel(x_ref, g_ref, wT_ref, qg_ref, o_ref, h_ref):
    _proj_prologue(x_ref, g_ref, h_ref)
    acc = _dot_nt(wT_ref[...], h_ref[...])
    tn, tm = acc.shape
    a3 = acc.reshape(tn // QK_DIM, QK_DIM, tm)
    ms = jnp.mean(a3 * a3, axis=1, keepdims=True)
    qn = (a3 * lax.rsqrt(ms + NORM_EPS)).reshape(tn, tm) * qg_ref[...]
    o_ref[0] = qn.astype(o_ref.dtype)


def _proj_k_kernel(x_ref, g_ref, w_ref, kg_ref, bd_ref, o_ref, h_ref):
    _proj_prologue(x_ref, g_ref, h_ref)
    acc = _dot(h_ref[...], w_ref[...])
    ss = _dot((acc * acc).astype(BF16), bd_ref[...])
    kn = acc * lax.rsqrt(ss * (1.0 / QK_DIM) + NORM_EPS) * kg_ref[...]
    o_ref[...] = kn.astype(o_ref.dtype)


def _proj_vT_kernel(x_ref, g_ref, wT_ref, o_ref, h_ref):
    _proj_prologue(x_ref, g_ref, h_ref)
    o_ref[0] = _dot_nt(wT_ref[...], h_ref[...]).astype(o_ref.dtype)


def _proj_rec_kernel(x_ref, g_ref, w_ref, o_ref, h_ref, *, blocks_per_seg):
    _proj_prologue(x_ref, g_ref, h_ref)
    acc = _dot(h_ref[...], w_ref[...])
    seg = pl.program_id(1) // blocks_per_seg
    is_silu = jnp.logical_or(seg == 0, seg == 4)
    o_ref[...] = jnp.where(is_silu, acc * (1.0 / (1.0 + jnp.exp(-acc))), acc)


def _projections(x2d, norm_g, w_in, q_norm_g, k_norm_g, tm):
    n_tok = x2d.shape[0]
    nb = n_tok // tm
    g_row = norm_g.reshape(1, D_MODEL)
    w_bf = w_in.astype(BF16)
    x_spec = pl.BlockSpec((tm, D_MODEL), lambda i, j: (i, 0))
    g_spec = pl.BlockSpec((1, D_MODEL), lambda i, j: (0, 0))
    scratch = [pltpu.VMEM((tm, D_MODEL), BF16)]
    sem = ("parallel", "arbitrary")

    tn = 512
    wqT = w_bf[:, 0:ATTN_W].T
    qg_col = jnp.tile(q_norm_g.astype(F32) * (QK_DIM ** -0.5 * LOG2E), 2 * HEADS).reshape(ATTN_W, 1)
    qT = pl.pallas_call(
        _proj_qT_kernel,
        grid=(nb, ATTN_W // tn),
        in_specs=[x_spec, g_spec,
                  pl.BlockSpec((tn, D_MODEL), lambda i, j: (j, 0)),
                  pl.BlockSpec((tn, 1), lambda i, j: (j, 0))],
        out_specs=pl.BlockSpec((1, tn, tm), lambda i, j: (i, j, 0)),
        out_shape=jax.ShapeDtypeStruct((nb, ATTN_W, tm), BF16),
        scratch_shapes=scratch, compiler_params=_cparams(sem), name="proj_qT",
    )(x2d, g_row, wqT, qg_col)

    tnk = 256
    kg_row = jnp.tile(k_norm_g.astype(F32), 2 * HEADS).reshape(1, ATTN_W)
    grp = np.arange(tnk) // QK_DIM
    bd = jnp.asarray((grp[:, None] == grp[None, :]).astype(np.float32), dtype=BF16)
    k = pl.pallas_call(
        _proj_k_kernel,
        grid=(nb, ATTN_W // tnk),
        in_specs=[x_spec, g_spec,
                  pl.BlockSpec((D_MODEL, tnk), lambda i, j: (0, j)),
                  pl.BlockSpec((1, tnk), lambda i, j: (0, j)),
                  pl.BlockSpec((tnk, tnk), lambda i, j: (0, 0))],
        out_specs=pl.BlockSpec((tm, tnk), lambda i, j: (i, j)),
        out_shape=jax.ShapeDtypeStruct((n_tok, ATTN_W), BF16),
        scratch_shapes=scratch, compiler_params=_cparams(sem), name="proj_k",
    )(x2d, g_row, w_bf[:, ATTN_W:2 * ATTN_W], kg_row, bd)

    wvT = w_bf[:, 2 * ATTN_W:3 * ATTN_W].T
    vT = pl.pallas_call(
        _proj_vT_kernel,
        grid=(nb, ATTN_W // tn),
        in_specs=[x_spec, g_spec, pl.BlockSpec((tn, D_MODEL), lambda i, j: (j, 0))],
        out_specs=pl.BlockSpec((1, tn, tm), lambda i, j: (i, j, 0)),
        out_shape=jax.ShapeDtypeStruct((nb, ATTN_W, tm), BF16),
        scratch_shapes=scratch, compiler_params=_cparams(sem), name="proj_vT",
    )(x2d, g_row, wvT)

    rec_w = 5 * REC_W
    tnr = REC_W
    rec = pl.pallas_call(
        functools.partial(_proj_rec_kernel, blocks_per_seg=REC_W // tnr),
        grid=(nb, rec_w // tnr),
        in_specs=[x_spec, g_spec, pl.BlockSpec((D_MODEL, tnr), lambda i, j: (0, j))],
        out_specs=pl.BlockSpec((tm, tnr), lambda i, j: (i, j)),
        out_shape=jax.ShapeDtypeStruct((n_tok, rec_w), F32),
        scratch_shapes=scratch, compiler_params=_cparams(sem), name="proj_rec",
    )(x2d, g_row, w_bf[:, 3 * ATTN_W:])
    return qT, k, vT, rec


L_ROWS = 16
SKIP_LOG2_MARGIN = 160.0
SKIP_NORM_SLACK = 1.02


def _attn_kernel(slopes_ref, qT_ref, k_ref, vT_ref, lam_ref, g_ref, o_ref,
                 m_ref, acc_ref, bias_ref, ta_ref, tb_ref, mta_ref, mtb_ref, kn_ref, *, nk, tk, tq):
    h = pl.program_id(1)
    i = pl.program_id(2)
    sl = slopes_ref[h]
    lane_h = lax.broadcasted_iota(I32, (1, HEAD_W), 1)
    lane_v = lax.broadcasted_iota(I32, (1, LANES), 1)

    @pl.when(i == 0)
    def _():
        rel = (lax.broadcasted_iota(I32, (tk, tq), 1)
               - lax.broadcasted_iota(I32, (tk, tq), 0)).astype(F32)
        sr = sl * rel
        sa = sl * jnp.abs(rel)
        bias_ref[0] = jnp.concatenate([sr, sr], axis=1)
        bias_ref[1] = jnp.concatenate([sa, sa], axis=1)
        bias_ref[2] = jnp.concatenate([-sr, -sr], axis=1)

        def block_norm(j, carry):
            kf = k_ref[pl.ds(pl.multiple_of(j * tk, tk), tk), :].astype(F32)
            sq = kf * kf
            out = []
            for c in range(2):
                in_map = (lane_h >= c * QK_DIM) & (lane_h < (c + 1) * QK_DIM)
                n2 = jnp.sum(jnp.where(in_map, sq, 0.0), axis=1, keepdims=True)
                nmax = jnp.sqrt(jnp.max(n2, axis=0, keepdims=True))
                out.append(jnp.where(lane_v == j, nmax, carry[c]))
            return tuple(out)

        kn0, kn1 = lax.fori_loop(0, nk, block_norm,
                                 (jnp.zeros((1, LANES), F32), jnp.zeros((1, LANES), F32)))
        kn_ref[0:1, :] = kn0
        kn_ref[1:2, :] = kn1

    qT = qT_ref[0]

    qf = qT.astype(F32)
    qsq = qf * qf
    thr = None
    for c in range(2):
        rows_c = slice(c * QK_DIM, (c + 1) * QK_DIM)
        bq = jnp.sqrt(jnp.max(jnp.sum(qsq[rows_c], axis=0, keepdims=True), axis=1, keepdims=True))
        kn = kn_ref[c:c + 1, :]
        k_all = jnp.max(kn, axis=1, keepdims=True)
        k_diag = jnp.max(jnp.where(lane_v == i, kn, 0.0), axis=1, keepdims=True)
        thr_c = SKIP_NORM_SLACK * bq * (k_all + k_diag) + SKIP_LOG2_MARGIN
        thr = thr_c if thr is None else jnp.maximum(thr, thr_c)
    dist = lane_v.astype(F32)
    visit = (lane_v >= 1) & (sl * ((dist - 1.0) * tq + 1.0) <= thr)
    radius = jnp.sum(visit.astype(I32))
    lo = jnp.maximum(i - radius, 0)
    hi = jnp.minimum(i + radius, nk - 1)
    n_vis = hi - lo + 1
    zero_half = jnp.zeros((QK_DIM, tq), qT.dtype)
    q_cat = jnp.concatenate([jnp.concatenate([qT[:QK_DIM], zero_half], axis=0),
                             jnp.concatenate([zero_half, qT[QK_DIM:]], axis=0)], axis=1)
    ones = jnp.ones((L_ROWS, tk), BF16)
    m_ref[...] = jnp.full(m_ref.shape, -jnp.inf, F32)
    acc_ref[...] = jnp.zeros(acc_ref.shape, F32)

    def shift_of(j):
        gap = jnp.abs(jnp.full((1, 2 * tq), (i - j) * tq, I32)).astype(F32)
        return -sl * gap

    def scores(j, t_ref, mt_ref):
        kblk = k_ref[pl.ds(pl.multiple_of(j * tk, tk), tk), :]
        side = jnp.where(j < i, 0, jnp.where(j == i, 1, 2))
        t = _dot(kblk, q_cat) - bias_ref[side]
        t_ref[...] = t
        mt_ref[...] = jnp.max(t, axis=0, keepdims=True) + shift_of(j)

    def accumulate(j, t_ref, mt_ref):
        m_old = m_ref[...]
        m_new = jnp.maximum(m_old, mt_ref[...])
        p = jnp.exp2(t_ref[...] - (m_new - shift_of(j)))
        v_aug = jnp.concatenate([vT_ref[j], ones], axis=0)
        acc_ref[...] = jnp.exp2(m_old - m_new) * acc_ref[...] + _dot(v_aug, p.astype(BF16))
        m_ref[...] = m_new

    def pair(jj, carry):
        j = lo + 2 * jj
        scores(j + 1, tb_ref, mtb_ref)
        accumulate(j, ta_ref, mta_ref)
        scores(jnp.minimum(j + 2, hi), ta_ref, mta_ref)
        accumulate(j + 1, tb_ref, mtb_ref)
        return carry

    scores(lo, ta_ref, mta_ref)
    lax.fori_loop(0, jnp.right_shift(n_vis, 1), pair, 0)

    @pl.when(jnp.bitwise_and(n_vis, 1) == 1)
    def _():
        accumulate(hi, ta_ref, mta_ref)

    lp = lam_ref[...]
    lam = (jnp.exp(jnp.sum(lp[0:1] * lp[1:2], axis=-1, keepdims=True))
           - jnp.exp(jnp.sum(lp[2:3] * lp[3:4], axis=-1, keepdims=True)) + LAM_INIT)
    acc = acc_ref[...]
    a0 = acc[:, :tq]
    a1 = acc[:, tq:]
    o = (a0[:V_DIM] / a0[V_DIM:V_DIM + 1]
         - lam * (a1[:V_DIM] / a1[V_DIM:V_DIM + 1]))
    ms = jnp.mean(o * o, axis=0, keepdims=True)
    on = o * lax.rsqrt(ms + NORM_EPS) * g_ref[...] * (1.0 - LAM_INIT)
    o_ref[...] = on.T.astype(o_ref.dtype)


def _attention(qT, k, vT, diff_lambda, diff_norm_g, batch, seq, tq):
    nq = seq // tq
    assert nq <= LANES, "per-block key norms are kept one per lane"
    slopes = LOG2E * jnp.exp2(-8.0 * jnp.arange(1, HEADS + 1, dtype=F32) / HEADS)
    g_col = diff_norm_g.astype(F32).reshape(ATTN_W, 1)
    grid_spec = pltpu.PrefetchScalarGridSpec(
        num_scalar_prefetch=1,
        grid=(batch, HEADS, nq),
        in_specs=[
            pl.BlockSpec((1, HEAD_W, tq), lambda b, h, i, s: (b * nq + i, h, 0)),
            pl.BlockSpec((seq, HEAD_W), lambda b, h, i, s: (b, h)),
            pl.BlockSpec((nq, HEAD_W, tq), lambda b, h, i, s: (b, h, 0)),
            pl.BlockSpec((4, QK_DIM), lambda b, h, i, s: (0, 0)),
            pl.BlockSpec((HEAD_W, 1), lambda b, h, i, s: (h, 0)),
        ],
        out_specs=pl.BlockSpec((tq, HEAD_W), lambda b, h, i, s: (b * nq + i, h)),
        scratch_shapes=[pltpu.VMEM((1, 2 * tq), F32),
                        pltpu.VMEM((V_DIM + L_ROWS, 2 * tq), F32),
                        pltpu.VMEM((3, tq, 2 * tq), F32),
                        pltpu.VMEM((tq, 2 * tq), F32), pltpu.VMEM((tq, 2 * tq), F32),
                        pltpu.VMEM((1, 2 * tq), F32), pltpu.VMEM((1, 2 * tq), F32),
                        pltpu.VMEM((2, LANES), F32)],
    )
    return pl.pallas_call(
        functools.partial(_attn_kernel, nk=nq, tk=tq, tq=tq),
        grid_spec=grid_spec,
        out_shape=jax.ShapeDtypeStruct((batch * seq, ATTN_W), BF16),
        compiler_params=_cparams(("arbitrary", "arbitrary", "arbitrary")), name="diff_attn",
    )(slopes, qT, k, vT, diff_lambda.astype(F32), g_col)


def _rec_mats(reverse):
    c = REC_CHUNK
    t = np.arange(c)[:, None]
    u = np.arange(c)[None, :]
    mats = [(u >= t) if reverse else (u <= t)]
    for lv in REC_LEVELS:
        if reverse:
            bd = (t // (2 * lv)) * (2 * lv) + lv
            mats.append(u >= bd)
        else:
            bd = (t // (2 * lv)) * (2 * lv) + lv - 1
            mats.append(u <= bd)
    return jnp.asarray(np.concatenate(mats, axis=0).astype(np.float32), dtype=BF16)


def _rec_gates(q_ref, z_ref, v_ref, lbp_ref, mats_ref, n_chunks):
    c = REC_CHUNK
    lbp = lbp_ref[...]
    mx = jnp.maximum(lbp[0:1], lbp[1:2])
    e0 = jnp.exp(lbp[0:1] - mx)
    e1 = jnp.exp(lbp[1:2] - mx)
    lb = e0 / (e0 + e1)

    rows = [slice(n * c, (n + 1) * c) for n in range(n_chunks)]
    qs = [q_ref[sl, :] for sl in rows]
    vs = [v_ref[sl, :].astype(BF16) for sl in rows]
    gs, kks = [], []
    for sl in rows:
        z = z_ref[sl, :]
        sig = 1.0 / (1.0 + jnp.exp(-z))
        gs.append(jnp.log(lb + (1.0 - lb) * sig))
        kks.append((1.0 - lb) * (1.0 / (1.0 + jnp.exp(z))))

    g_cat = jnp.concatenate(gs, axis=1)
    g1 = g_cat.astype(BF16)
    r1 = g_cat - g1.astype(F32)
    g2 = r1.astype(BF16)
    g3 = (r1 - g2.astype(F32)).astype(BF16)
    mats = mats_ref[...]
    stacked = (_dot(mats, g3) + _dot(mats, g2)) + _dot(mats, g1)
    return qs, kks, vs, stacked


def _rec_scores(qs, kks, vs, stacked, reverse):
    c = REC_CHUNK
    n_chunks = len(qs)
    row = lax.broadcasted_iota(I32, (c, HEAD_W), 0)
    ti = lax.broadcasted_iota(I32, (c, c), 0)
    si = lax.broadcasted_iota(I32, (c, c), 1)
    is_query, same = [], []
    for lv in REC_LEVELS:
        in_upper = (row & (2 * lv - 1)) >= lv
        is_query.append(jnp.logical_not(in_upper) if reverse else in_upper)
        shift = (2 * lv).bit_length() - 1
        same.append((ti >> shift) == (si >> shift))
    diag = ti == si

    o_intra, incs, qes, decays = [], [], [], []
    for n in range(n_chunks):
        lanes = slice(n * HEAD_W, (n + 1) * HEAD_W)
        b = stacked[0:c, lanes]
        q, kk = qs[n], kks[n]
        a = jnp.where(diag, _dot_nt(q.astype(BF16), kk.astype(BF16)), 0.0)
        for m, lv in enumerate(REC_LEVELS):
            r = stacked[(m + 1) * c:(m + 2) * c, lanes]
            e = jnp.exp(-jnp.abs(b - r))
            qt = jnp.where(is_query[m], q * e, 0.0).astype(BF16)
            kt = jnp.where(is_query[m], 0.0, kk * e).astype(BF16)
            a = a + jnp.where(same[m], _dot_nt(qt, kt), 0.0)
        b_end = b[0:1] if reverse else b[c - 1:c]
        o_intra.append(_dot(a.astype(BF16), vs[n]))
        incs.append(_dot_tn(vs[n], (kk * jnp.exp(b_end - b)).astype(BF16)))
        qes.append((q * jnp.exp(b)).astype(BF16))
        decays.append(jnp.exp(b_end))
    return o_intra, incs, qes, decays


def _rec_scan(o_intra, incs, qes, decays, o_ref, st_ref, reverse):
    n_chunks = len(o_intra)
    st = st_ref[...]
    order = range(n_chunks - 1, -1, -1) if reverse else range(n_chunks)
    for n in order:
        o_ref[n * REC_CHUNK:(n + 1) * REC_CHUNK, :] = o_intra[n] + _dot_nt(qes[n], st.astype(BF16))
        st = st * decays[n] + incs[n]
    st_ref[...] = st


def _rec_kernel(qf_ref, zf_ref, vf_ref, qb_ref, zb_ref, vb_ref, lbpf_ref, lbpb_ref, matsf_ref,
                matsb_ref, of_ref, ob_ref, stf_ref, stb_ref, *, n_chunks):
    @pl.when(pl.program_id(2) == 0)
    def _():
        stf_ref[...] = jnp.zeros(stf_ref.shape, F32)
        stb_ref[...] = jnp.zeros(stb_ref.shape, F32)

    gates_f = _rec_gates(qf_ref, zf_ref, vf_ref, lbpf_ref, matsf_ref, n_chunks)
    gates_b = _rec_gates(qb_ref, zb_ref, vb_ref, lbpb_ref, matsb_ref, n_chunks)
    parts_f = _rec_scores(*gates_f, False)
    parts_b = _rec_scores(*gates_b, True)
    _rec_scan(*parts_f, of_ref, stf_ref, False)
    _rec_scan(*parts_b, ob_ref, stb_ref, True)


def _recurrence(rec, rec_lower_bound, batch, seq, tb):
    nb = seq // tb
    lbp = rec_lower_bound.astype(F32)

    def fwd(col):
        return lambda b, h, i: (b * nb + i, col + h)

    def bwd(col):
        return lambda b, h, i: (b * nb + nb - 1 - i, col + h)

    blk = (tb, HEAD_W)
    mat_shape = ((len(REC_LEVELS) + 1) * REC_CHUNK, REC_CHUNK)
    out = jax.ShapeDtypeStruct((batch * seq, REC_W), F32)
    return pl.pallas_call(
        functools.partial(_rec_kernel, n_chunks=tb // REC_CHUNK),
        grid=(batch, HEADS, nb),
        in_specs=[
            pl.BlockSpec(blk, fwd(0)), pl.BlockSpec(blk, fwd(HEADS)), pl.BlockSpec(blk, fwd(3 * HEADS)),
            pl.BlockSpec(blk, bwd(0)), pl.BlockSpec(blk, bwd(2 * HEADS)), pl.BlockSpec(blk, bwd(3 * HEADS)),
            pl.BlockSpec((2, HEAD_W), lambda b, h, i: (0, h)),
            pl.BlockSpec((2, HEAD_W), lambda b, h, i: (0, h)),
            pl.BlockSpec(mat_shape, lambda b, h, i: (0, 0)),
            pl.BlockSpec(mat_shape, lambda b, h, i: (0, 0)),
        ],
        out_specs=[pl.BlockSpec(blk, fwd(0)), pl.BlockSpec(blk, bwd(0))],
        out_shape=[out, out],
        scratch_shapes=[pltpu.VMEM((HEAD_W, HEAD_W), F32), pltpu.VMEM((HEAD_W, HEAD_W), F32)],
        compiler_params=_cparams(("parallel", "parallel", "arbitrary")), name="rec_bidir",
    )(rec, rec, rec, rec, rec, rec, lbp[0], lbp[1], _rec_mats(False), _rec_mats(True))


def _split_bf16(x):
    hi = x.astype(BF16)
    lo = (x - hi.astype(F32)).astype(BF16)
    return hi, lo


def _outproj_kernel(x_ref, ao_ref, of_ref, ob_ref, rg_ref, rng_ref, woa_ref, wor_ref,
                    fg_ref, wrh_ref, wrl_ref, x1_ref, x1acc_ref, aff_ref):
    parts = []
    for hd in range(HEADS):
        sl = slice(hd * HEAD_W, (hd + 1) * HEAD_W)
        o = of_ref[:, sl] + ob_ref[:, sl]
        parts.append((_rms(o, rng_ref[:, sl]) * rg_ref[:, sl]).astype(BF16))
    ro = jnp.concatenate(parts, axis=-1)
    x1 = x_ref[...] + _dot(ao_ref[...], woa_ref[...]) + _dot(ro, wor_ref[...])
    x1_ref[...] = x1
    x1acc_ref[...] = x1
    h2 = _rms(x1, fg_ref[...])
    hi, lo = _split_bf16(h2)
    logits = _dot(hi, wrh_ref[...]) + _dot(lo, wrh_ref[...]) + _dot(hi, wrl_ref[...])
    lane = lax.broadcasted_iota(I32, logits.shape, 1)
    logits = jnp.where(lane < N_EXPERTS, logits, -jnp.inf)
    mx = jnp.max(logits, axis=-1, keepdims=True)
    ex = jnp.exp(logits - mx)
    aff = ex / jnp.sum(ex, axis=-1, keepdims=True)
    aff_ref[...] = aff.T[:N_EXPERTS]


def _outproj(x2d, ao, o_f, o_b, rec, rec_norm_g, w_out, norm_ffn_g, w_router, tm):
    n_tok = x2d.shape[0]
    w_bf = w_out.astype(BF16)
    wr = jnp.pad(w_router.astype(F32), ((0, 0), (0, LANES - N_EXPERTS)))
    wr_hi = wr.astype(BF16)
    wr_lo = (wr - wr_hi.astype(F32)).astype(BF16)
    row = lambda i: (i, 0)
    fixed = lambda i: (0, 0)
    return pl.pallas_call(
        _outproj_kernel,
        grid=(n_tok // tm,),
        in_specs=[
            pl.BlockSpec((tm, D_MODEL), row),
            pl.BlockSpec((tm, ATTN_W), row),
            pl.BlockSpec((tm, REC_W), row),
            pl.BlockSpec((tm, REC_W), row),
            pl.BlockSpec((tm, REC_W), lambda i: (i, 4)),
            pl.BlockSpec((1, REC_W), fixed),
            pl.BlockSpec((ATTN_W, D_MODEL), fixed),
            pl.BlockSpec((REC_W, D_MODEL), fixed),
            pl.BlockSpec((1, D_MODEL), fixed),
            pl.BlockSpec((D_MODEL, LANES), fixed),
            pl.BlockSpec((D_MODEL, LANES), fixed),
        ],
        out_specs=[pl.BlockSpec((tm, D_MODEL), row), pl.BlockSpec((tm, D_MODEL), row),
                   pl.BlockSpec((N_EXPERTS, tm), lambda i: (0, i))],
        out_shape=[jax.ShapeDtypeStruct((n_tok, D_MODEL), F32),
                   jax.ShapeDtypeStruct((n_tok, D_MODEL), F32),
                   jax.ShapeDtypeStruct((N_EXPERTS, n_tok), F32)],
        compiler_params=_cparams(("parallel",)), name="outproj_router",
    )(x2d, ao, o_f, o_b, rec, rec_norm_g.astype(F32).reshape(1, REC_W),
      w_bf[:ATTN_W], w_bf[ATTN_W:], norm_ffn_g.astype(F32).reshape(1, D_MODEL), wr_hi, wr_lo)


TOPK_BLK = 256


def _topk_kernel(aff_ref, tri_ref, idx_ref, gate_ref, *, n_tok, cap):
    aff = aff_ref[...]

    def search(step, cand):
        trial = cand | (jnp.int32(1) << (30 - step))
        n_ge = jnp.sum((aff >= pltpu.bitcast(trial, F32)).astype(I32), axis=1, keepdims=True)
        return jnp.where(n_ge >= cap, trial, cand)

    thr = pltpu.bitcast(lax.fori_loop(0, 31, search, jnp.zeros((N_EXPERTS, 1), I32)), F32)
    gt = aff > thr
    tie = aff == thr
    need = (cap - jnp.sum(gt.astype(I32), axis=1, keepdims=True)).astype(F32)

    def running_count(flags):
        f = flags.astype(F32).astype(BF16)
        carry = jnp.zeros((N_EXPERTS, 1), F32)
        out = []
        for j in range(n_tok // TOPK_BLK):
            pre = _dot(f[:, j * TOPK_BLK:(j + 1) * TOPK_BLK], tri_ref[...]) + carry
            out.append(pre)
            carry = pre[:, TOPK_BLK - 1:TOPK_BLK]
        return jnp.concatenate(out, axis=1)

    sel = jnp.logical_or(gt, jnp.logical_and(tie, running_count(tie) <= need))
    slot = running_count(sel).astype(I32) - 1

    lane = lax.broadcasted_iota(I32, (N_EXPERTS, n_tok), 1)
    packed = jnp.where(sel, (slot << 16) | lane, -1)
    val = aff
    for k in range((n_tok - 1).bit_length()):
        step = 1 << k
        dist = lane - (packed >> 16)
        moving = jnp.logical_and(packed >= 0, ((dist >> k) & 1) == 1)
        arriving = pltpu.roll(jnp.where(moving, packed, -1), n_tok - step, axis=1)
        arriving_val = pltpu.roll(val, n_tok - step, axis=1)
        lands = arriving >= 0
        packed = jnp.where(lands, arriving, jnp.where(moving, -1, packed))
        val = jnp.where(lands, arriving_val, val)
    idx_ref[...] = packed[:, :cap] & 0xFFFF
    gate_ref[...] = val[:, :cap]


def _topk(aff_t, cap):
    n_tok = aff_t.shape[1]
    assert n_tok <= 32768 and n_tok % TOPK_BLK == 0, "token index and slot share one int32"
    r = np.arange(TOPK_BLK)
    tri = jnp.asarray((r[:, None] <= r[None, :]).astype(np.float32), dtype=BF16)
    return pl.pallas_call(
        functools.partial(_topk_kernel, n_tok=n_tok, cap=cap),
        out_shape=[jax.ShapeDtypeStruct((N_EXPERTS, cap), I32),
                   jax.ShapeDtypeStruct((N_EXPERTS, cap), F32)],
        compiler_params=pltpu.CompilerParams(vmem_limit_bytes=VMEM_LIMIT), name="expert_topk",
    )(aff_t, tri)


ROW_DMA_UNROLL = 8


def _row_copy(hbm, vmem, t, p, sem, gather):
    if gather:
        return pltpu.make_async_copy(hbm.at[pl.ds(t, 1), :], vmem.at[pl.ds(p, 1), :], sem)
    return pltpu.make_async_copy(vmem.at[pl.ds(p, 1), :], hbm.at[pl.ds(t, 1), :], sem)


def _rows_start_loop(hbm, vmem, idx_ref, base, cap, sem, gather):
    def issue(blk, carry):
        for u in range(ROW_DMA_UNROLL):
            p = blk * ROW_DMA_UNROLL + u
            _row_copy(hbm, vmem, idx_ref[base + p], p, sem, gather).start()
        return carry
    lax.fori_loop(0, cap // ROW_DMA_UNROLL, issue, 0)


def _rows_start_inline(hbm, vmem, idx_ref, base, p0, count, sem, gather):
    for u in range(count):
        _row_copy(hbm, vmem, idx_ref[base + p0 + u], p0 + u, sem, gather).start()


def _rows_wait(hbm, vmem, cap, sem, gather):
    if gather:
        pltpu.make_async_copy(hbm.at[pl.ds(0, cap), :], vmem, sem).wait()
    else:
        pltpu.make_async_copy(vmem, hbm.at[pl.ds(0, cap), :], sem).wait()


def _moe_kernel(idx_ref, gate_ref, fg_ref, x_hbm, acc_in_hbm, wg_ref, wu_ref, wd_ref, out_hbm,
                xg, rows, xe, ye, sem_x, sem_r, sem_s, *, cap, nf):
    del acc_in_hbm
    e = pl.program_id(0)
    f = pl.program_id(1)
    last_e = pl.num_programs(0) - 1
    chunk = cap // nf

    @pl.when(f == 0)
    def _():
        @pl.when(e == 0)
        def _():
            _rows_start_loop(x_hbm, xg, idx_ref, 0, cap, sem_x, True)
            _rows_wait(x_hbm, xg, cap, sem_x, True)

        @pl.when(e > 0)
        def _():
            _rows_wait(out_hbm, rows, cap, sem_s, False)

        xe[...] = _rms(xg[...], fg_ref[...]).astype(BF16)
        ye[...] = jnp.zeros(ye.shape, F32)

    _rows_start_inline(x_hbm, xg, idx_ref, jnp.minimum(e + 1, last_e) * cap, f * chunk, chunk,
                       sem_x, True)
    _rows_start_inline(out_hbm, rows, idx_ref, e * cap, f * chunk, chunk, sem_r, True)

    xb = xe[...]
    hg = _dot(xb, wg_ref[0].astype(BF16))
    hu = _dot(xb, wu_ref[0].astype(BF16))
    hid = hg * (1.0 / (1.0 + jnp.exp(-hg))) * hu
    ye[...] += _dot(hid.astype(BF16), wd_ref[0].astype(BF16))

    @pl.when(f == nf - 1)
    def _():
        _rows_wait(x_hbm, xg, cap, sem_x, True)
        _rows_wait(out_hbm, rows, cap, sem_r, True)
        rows[...] = rows[...] + ye[...] * gate_ref[0]
        _rows_start_inline(out_hbm, rows, idx_ref, e * cap, 0, cap, sem_s, False)

        @pl.when(e == last_e)
        def _():
            _rows_wait(out_hbm, rows, cap, sem_s, False)


def _moe(x1, x1_acc, idx, gates, norm_ffn_g, w_gate, w_up, w_down, tf):
    n_tok = x1.shape[0]
    cap = idx.shape[1]
    nf = D_EXPERT // tf
    grid_spec = pltpu.PrefetchScalarGridSpec(
        num_scalar_prefetch=1,
        grid=(N_EXPERTS, nf),
        in_specs=[
            pl.BlockSpec((1, cap, 1), lambda e, f, s: (e, 0, 0)),
            pl.BlockSpec((1, D_MODEL), lambda e, f, s: (0, 0)),
            pl.BlockSpec(memory_space=pl.ANY),
            pl.BlockSpec(memory_space=pl.ANY),
            pl.BlockSpec((1, D_MODEL, tf), lambda e, f, s: (e, 0, f)),
            pl.BlockSpec((1, D_MODEL, tf), lambda e, f, s: (e, 0, f)),
            pl.BlockSpec((1, tf, D_MODEL), lambda e, f, s: (e, f, 0)),
        ],
        out_specs=pl.BlockSpec(memory_space=pl.ANY),
        scratch_shapes=[pltpu.VMEM((cap, D_MODEL), F32), pltpu.VMEM((cap, D_MODEL), F32),
                        pltpu.VMEM((cap, D_MODEL), BF16), pltpu.VMEM((cap, D_MODEL), F32),
                        pltpu.SemaphoreType.DMA(()), pltpu.SemaphoreType.DMA(()),
                        pltpu.SemaphoreType.DMA(())],
    )
    return pl.pallas_call(
        functools.partial(_moe_kernel, cap=cap, nf=nf),
        grid_spec=grid_spec,
        out_shape=jax.ShapeDtypeStruct((n_tok, D_MODEL), F32),
        input_output_aliases={4: 0},
        compiler_params=_cparams(("arbitrary", "arbitrary")), name="expert_ffn",
    )(idx.reshape(N_EXPERTS * cap), gates.reshape(N_EXPERTS, cap, 1),
      norm_ffn_g.astype(F32).reshape(1, D_MODEL),
      x1, x1_acc, w_gate, w_up, w_down)


def _ple_kernel(x_ref, p_ref, g_ref, wg_ref, wp_ref, o_ref):
    x = x_ref[...]
    zg = _dot(_rms(x, g_ref[...]).astype(BF16), wg_ref[...])
    gate = 1.0 / (1.0 + jnp.exp(-zg))
    o_ref[...] = x + gate * _dot(p_ref[...].astype(BF16), wp_ref[...])


def _ple(x2, p2d, norm_ple_g, w_ple_gate, w_ple_proj, tm):
    n_tok = x2.shape[0]
    row = lambda i: (i, 0)
    fixed = lambda i: (0, 0)
    return pl.pallas_call(
        _ple_kernel,
        grid=(n_tok // tm,),
        in_specs=[pl.BlockSpec((tm, D_MODEL), row), pl.BlockSpec((tm, PLE_DIM), row),
                  pl.BlockSpec((1, D_MODEL), fixed), pl.BlockSpec((D_MODEL, D_MODEL), fixed),
                  pl.BlockSpec((PLE_DIM, D_MODEL), fixed)],
        out_specs=pl.BlockSpec((tm, D_MODEL), row),
        out_shape=jax.ShapeDtypeStruct((n_tok, D_MODEL), F32),
        compiler_params=_cparams(("parallel",)), name="ple_gate",
    )(x2, p2d, norm_ple_g.astype(F32).reshape(1, D_MODEL), w_ple_gate.astype(BF16),
      w_ple_proj.astype(BF16))


def _tiles(seq):
    t_attn = min(512, seq)
    t_rec = min(512, seq)
    t_row = min(256, seq)
    return t_attn, t_rec, t_row


def _layer(x, p, norm_mix_g, w_in, q_norm_g, k_norm_g, diff_lambda, diff_norm_g, rec_lower_bound,
           rec_norm_g, w_out, norm_ffn_g, w_router, w_expert_gate, w_expert_up, w_expert_down,
           norm_ple_g, w_ple_gate, w_ple_proj):
    batch, seq, _ = x.shape
    n_tok = batch * seq
    t_attn, t_rec, t_row = _tiles(seq)
    x2d = x.reshape(n_tok, D_MODEL)
    qT, k, vT, rec = _projections(x2d, norm_mix_g[0], w_in[0], q_norm_g[0], k_norm_g[0], t_attn)
    ao = _attention(qT, k, vT, diff_lambda[0], diff_norm_g[0], batch, seq, t_attn)
    o_f, o_b = _recurrence(rec, rec_lower_bound, batch, seq, t_rec)
    x1, x1_acc, aff = _outproj(x2d, ao, o_f, o_b, rec, rec_norm_g[0], w_out[0], norm_ffn_g[0],
                               w_router[0], t_row)
    cap = max(1, (CAPACITY_FACTOR * n_tok) // N_EXPERTS)
    idx, gates = _topk(aff, cap)
    x2 = _moe(x1, x1_acc, idx, gates, norm_ffn_g[0], w_expert_gate[0], w_expert_up[0],
              w_expert_down[0], tf=256)
    y = _ple(x2, p[0].reshape(n_tok, PLE_DIM), norm_ple_g[0], w_ple_gate[0], w_ple_proj[0], t_row)
    return y.reshape(x.shape)


def kernel(x_prompt, x_sample, p_prompt, p_sample, norm_mix_g, w_in, q_norm_g, k_norm_g, diff_lambda, diff_norm_g, rec_lower_bound, rec_norm_g, w_out, norm_ffn_g, w_router, w_expert_gate, w_expert_up, w_expert_down, norm_ple_g, w_ple_gate, w_ple_proj):
    weights = (norm_mix_g, w_in, q_norm_g, k_norm_g, diff_lambda, diff_norm_g, rec_lower_bound,
               rec_norm_g, w_out, norm_ffn_g, w_router, w_expert_gate, w_expert_up, w_expert_down,
               norm_ple_g, w_ple_gate, w_ple_proj)
    return (_layer(x_prompt, p_prompt, *weights), _layer(x_sample, p_sample, *weights))
```

```python
import functools
import math

import numpy as np
import jax
import jax.numpy as jnp
from jax import lax
from jax.experimental import pallas as pl
from jax.experimental.pallas import tpu as pltpu

F32 = jnp.float32
BF16 = jnp.bfloat16
I32 = jnp.int32

D_MODEL = 2048
PLE_DIM = 256
HEADS = 8
QK_DIM = 64
V_DIM = 128
HEAD_W = 128
ATTN_W = HEADS * V_DIM
REC_W = HEADS * V_DIM
N_EXPERTS = 16
CAPACITY_FACTOR = 2
D_EXPERT = 2048
NORM_EPS = 1e-6
LAM_INIT = 0.8 - 0.6 * math.exp(-0.3 * 0)

LANES = 128
LOG2E = 1.4426950408889634
VMEM_LIMIT = 56 * 1024 * 1024

REC_CHUNK = 64
REC_LEVELS = (1, 2, 4, 8, 16, 32)


def _cparams(sem, vmem=VMEM_LIMIT):
    return pltpu.CompilerParams(dimension_semantics=sem, vmem_limit_bytes=vmem)


def _dot(a, b):
    return jnp.dot(a, b, preferred_element_type=F32)


def _dot_nt(a, b):
    return lax.dot_general(a, b, (((1,), (1,)), ((), ())), preferred_element_type=F32)


def _dot_tn(a, b):
    return lax.dot_general(a, b, (((0,), (0,)), ((), ())), preferred_element_type=F32)


def _rms(x, g):
    ms = jnp.mean(x * x, axis=-1, keepdims=True)
    return x * lax.rsqrt(ms + NORM_EPS) * g


PROJ_TN = 512
QK_GROUP_TILE = 256
Q_BLKS = ATTN_W // PROJ_TN
REC_SEGS = 5


def _proj_kernel(x_ref, g_ref, w_ref, qg_ref, kg_ref, bd_ref, qT_ref, k_ref, vT_ref, rec_ref, h_ref,
                 *, tq):
    j = pl.program_id(1)

    @pl.when(j == 0)
    def _():
        h_ref[...] = _rms(x_ref[...], g_ref[...]).astype(BF16)

    acc = _dot(h_ref[...], w_ref[...])
    tm = acc.shape[0]

    def qk_norm(gain_row):
        parts = []
        for c0 in range(0, PROJ_TN, QK_GROUP_TILE):
            sub = acc[:, c0:c0 + QK_GROUP_TILE]
            ss = _dot((sub * sub).astype(BF16), bd_ref[...])
            parts.append(sub * lax.rsqrt(ss * (1.0 / QK_DIM) + NORM_EPS))
        return jnp.concatenate(parts, axis=1) * gain_row

    def store_transposed(o_ref, val):
        for r in range(tm // tq):
            o_ref[r] = val[r * tq:(r + 1) * tq, :].T.astype(o_ref.dtype)

    @pl.when(j < Q_BLKS)
    def _():
        store_transposed(qT_ref, qk_norm(qg_ref[...]))

    @pl.when(jnp.logical_and(j >= Q_BLKS, j < 2 * Q_BLKS))
    def _():
        k_ref[...] = qk_norm(kg_ref[...]).astype(k_ref.dtype)

    @pl.when(jnp.logical_and(j >= 2 * Q_BLKS, j < 3 * Q_BLKS))
    def _():
        store_transposed(vT_ref, acc)

    @pl.when(j >= 3 * Q_BLKS)
    def _():
        seg = (j - 3 * Q_BLKS) // Q_BLKS
        is_silu = jnp.logical_or(seg == 0, seg == REC_SEGS - 1)
        rec_ref[...] = jnp.where(is_silu, acc * (1.0 / (1.0 + jnp.exp(-acc))), acc)


def _projections(x2d, norm_g, w_in, q_norm_g, k_norm_g, tm, tq):
    n_tok = x2d.shape[0]
    qg_row = jnp.tile(q_norm_g.astype(F32) * (QK_DIM ** -0.5 * LOG2E), 2 * HEADS).reshape(1, ATTN_W)
    kg_row = jnp.tile(k_norm_g.astype(F32), 2 * HEADS).reshape(1, ATTN_W)
    grp = np.arange(QK_GROUP_TILE) // QK_DIM
    bd = jnp.asarray((grp[:, None] == grp[None, :]).astype(np.float32), dtype=BF16)
    n_col_blocks = (3 * ATTN_W + REC_SEGS * REC_W) // PROJ_TN

    def seg_block(first):
        def index(j, n_blocks):
            return jnp.clip(j - first, 0, n_blocks - 1)
        return index

    q_blk, k_blk, v_blk, r_blk = (seg_block(0), seg_block(Q_BLKS), seg_block(2 * Q_BLKS),
                                  seg_block(3 * Q_BLKS))
    t_shape = (tm // tq, PROJ_TN, tq)
    return pl.pallas_call(
        functools.partial(_proj_kernel, tq=tq),
        grid=(n_tok // tm, n_col_blocks),
        in_specs=[
            pl.BlockSpec((tm, D_MODEL), lambda i, j: (i, 0)),
            pl.BlockSpec((1, D_MODEL), lambda i, j: (0, 0)),
            pl.BlockSpec((D_MODEL, PROJ_TN), lambda i, j: (0, j)),
            pl.BlockSpec((1, PROJ_TN), lambda i, j: (0, q_blk(j, Q_BLKS))),
            pl.BlockSpec((1, PROJ_TN), lambda i, j: (0, k_blk(j, Q_BLKS))),
            pl.BlockSpec((QK_GROUP_TILE, QK_GROUP_TILE), lambda i, j: (0, 0)),
        ],
        out_specs=[
            pl.BlockSpec(t_shape, lambda i, j: (i, q_blk(j, Q_BLKS), 0)),
            pl.BlockSpec((tm, PROJ_TN), lambda i, j: (i, k_blk(j, Q_BLKS))),
            pl.BlockSpec(t_shape, lambda i, j: (i, v_blk(j, Q_BLKS), 0)),
            pl.BlockSpec((tm, PROJ_TN), lambda i, j: (i, r_blk(j, REC_SEGS * Q_BLKS))),
        ],
        out_shape=[
            jax.ShapeDtypeStruct((n_tok // tq, ATTN_W, tq), BF16),
            jax.ShapeDtypeStruct((n_tok, ATTN_W), BF16),
            jax.ShapeDtypeStruct((n_tok // tq, ATTN_W, tq), BF16),
            jax.ShapeDtypeStruct((n_tok, REC_SEGS * REC_W), F32),
        ],
        scratch_shapes=[pltpu.VMEM((tm, D_MODEL), BF16)],
        compiler_params=_cparams(("parallel", "arbitrary")), name="in_proj",
    )(x2d, norm_g.reshape(1, D_MODEL), w_in.astype(BF16), qg_row, kg_row, bd)


L_ROWS = 16
SKIP_LOG2_MARGIN = 160.0
SKIP_NORM_SLACK = 1.02


def _attn_kernel(slopes_ref, qT_ref, k_ref, vT_ref, lam_ref, g_ref, o_ref,
                 m_ref, acc_ref, bias_ref, ta_ref, tb_ref, mta_ref, mtb_ref, kn_ref, *, nk, tk, tq):
    h = pl.program_id(1)
    i = pl.program_id(2)
    sl = slopes_ref[h]
    lane_h = lax.broadcasted_iota(I32, (1, HEAD_W), 1)
    lane_v = lax.broadcasted_iota(I32, (1, LANES), 1)

    @pl.when(i == 0)
    def _():
        rel = (lax.broadcasted_iota(I32, (tk, tq), 1)
               - lax.broadcasted_iota(I32, (tk, tq), 0)).astype(F32)
        sr = sl * rel
        sa = sl * jnp.abs(rel)
        bias_ref[0] = jnp.concatenate([sr, sr], axis=1)
        bias_ref[1] = jnp.concatenate([sa, sa], axis=1)
        bias_ref[2] = jnp.concatenate([-sr, -sr], axis=1)

        def block_norm(j, carry):
            kf = k_ref[pl.ds(pl.multiple_of(j * tk, tk), tk), :].astype(F32)
            sq = kf * kf
            out = []
            for c in range(2):
                in_map = (lane_h >= c * QK_DIM) & (lane_h < (c + 1) * QK_DIM)
                n2 = jnp.sum(jnp.where(in_map, sq, 0.0), axis=1, keepdims=True)
                nmax = jnp.sqrt(jnp.max(n2, axis=0, keepdims=True))
                out.append(jnp.where(lane_v == j, nmax, carry[c]))
            return tuple(out)

        kn0, kn1 = lax.fori_loop(0, nk, block_norm,
                                 (jnp.zeros((1, LANES), F32), jnp.zeros((1, LANES), F32)))
        kn_ref[0:1, :] = kn0
        kn_ref[1:2, :] = kn1

    qT = qT_ref[0]

    qf = qT.astype(F32)
    qsq = qf * qf
    thr = None
    for c in range(2):
        rows_c = slice(c * QK_DIM, (c + 1) * QK_DIM)
        bq = jnp.sqrt(jnp.max(jnp.sum(qsq[rows_c], axis=0, keepdims=True), axis=1, keepdims=True))
        kn = kn_ref[c:c + 1, :]
        k_all = jnp.max(kn, axis=1, keepdims=True)
        k_diag = jnp.max(jnp.where(lane_v == i, kn, 0.0), axis=1, keepdims=True)
        thr_c = SKIP_NORM_SLACK * bq * (k_all + k_diag) + SKIP_LOG2_MARGIN
        thr = thr_c if thr is None else jnp.maximum(thr, thr_c)
    dist = lane_v.astype(F32)
    visit = (lane_v >= 1) & (sl * ((dist - 1.0) * tq + 1.0) <= thr)
    radius = jnp.sum(visit.astype(I32))
    lo = jnp.maximum(i - radius, 0)
    hi = jnp.minimum(i + radius, nk - 1)
    n_vis = hi - lo + 1
    zero_half = jnp.zeros((QK_DIM, tq), qT.dtype)
    q_cat = jnp.concatenate([jnp.concatenate([qT[:QK_DIM], zero_half], axis=0),
                             jnp.concatenate([zero_half, qT[QK_DIM:]], axis=0)], axis=1)
    ones = jnp.ones((L_ROWS, tk), BF16)
    m_ref[...] = jnp.full(m_ref.shape, -jnp.inf, F32)
    acc_ref[...] = jnp.zeros(acc_ref.shape, F32)

    def shift_of(j):
        gap = jnp.abs(jnp.full((1, 2 * tq), (i - j) * tq, I32)).astype(F32)
        return -sl * gap

    def scores(j, t_ref, mt_ref):
        kblk = k_ref[pl.ds(pl.multiple_of(j * tk, tk), tk), :]
        side = jnp.where(j < i, 0, jnp.where(j == i, 1, 2))
        t = _dot(kblk, q_cat) - bias_ref[side]
        t_ref[...] = t
        mt_ref[...] = jnp.max(t, axis=0, keepdims=True) + shift_of(j)

    def accumulate(j, t_ref, mt_ref):
        m_old = m_ref[...]
        m_new = jnp.maximum(m_old, mt_ref[...])
        p = jnp.exp2(t_ref[...] - (m_new - shift_of(j)))
        v_aug = jnp.concatenate([vT_ref[j], ones], axis=0)
        acc_ref[...] = jnp.exp2(m_old - m_new) * acc_ref[...] + _dot(v_aug, p.astype(BF16))
        m_ref[...] = m_new

    def pair(jj, carry):
        j = lo + 2 * jj
        scores(j + 1, tb_ref, mtb_ref)
        accumulate(j, ta_ref, mta_ref)
        scores(jnp.minimum(j + 2, hi), ta_ref, mta_ref)
        accumulate(j + 1, tb_ref, mtb_ref)
        return carry

    scores(lo, ta_ref, mta_ref)
    lax.fori_loop(0, jnp.right_shift(n_vis, 1), pair, 0)

    @pl.when(jnp.bitwise_and(n_vis, 1) == 1)
    def _():
        accumulate(hi, ta_ref, mta_ref)

    lp = lam_ref[...]
    lam = (jnp.exp(jnp.sum(lp[0:1] * lp[1:2], axis=-1, keepdims=True))
           - jnp.exp(jnp.sum(lp[2:3] * lp[3:4], axis=-1, keepdims=True)) + LAM_INIT)
    acc = acc_ref[...]
    a0 = acc[:, :tq]
    a1 = acc[:, tq:]
    o = (a0[:V_DIM] / a0[V_DIM:V_DIM + 1]
         - lam * (a1[:V_DIM] / a1[V_DIM:V_DIM + 1]))
    ms = jnp.mean(o * o, axis=0, keepdims=True)
    on = o * lax.rsqrt(ms + NORM_EPS) * g_ref[...] * (1.0 - LAM_INIT)
    o_ref[...] = on.T.astype(o_ref.dtype)


def _attention(qT, k, vT, diff_lambda, diff_norm_g, batch, seq, tq):
    nq = seq // tq
    assert nq <= LANES, "per-block key norms are kept one per lane"
    slopes = LOG2E * jnp.exp2(-8.0 * jnp.arange(1, HEADS + 1, dtype=F32) / HEADS)
    g_col = diff_norm_g.astype(F32).reshape(ATTN_W, 1)
    grid_spec = pltpu.PrefetchScalarGridSpec(
        num_scalar_prefetch=1,
        grid=(batch, HEADS, nq),
        in_specs=[
            pl.BlockSpec((1, HEAD_W, tq), lambda b, h, i, s: (b * nq + i, h, 0)),
            pl.BlockSpec((seq, HEAD_W), lambda b, h, i, s: (b, h)),
            pl.BlockSpec((nq, HEAD_W, tq), lambda b, h, i, s: (b, h, 0)),
            pl.BlockSpec((4, QK_DIM), lambda b, h, i, s: (0, 0)),
            pl.BlockSpec((HEAD_W, 1), lambda b, h, i, s: (h, 0)),
        ],
        out_specs=pl.BlockSpec((tq, HEAD_W), lambda b, h, i, s: (b * nq + i, h)),
        scratch_shapes=[pltpu.VMEM((1, 2 * tq), F32),
                        pltpu.VMEM((V_DIM + L_ROWS, 2 * tq), F32),
                        pltpu.VMEM((3, tq, 2 * tq), F32),
                        pltpu.VMEM((tq, 2 * tq), F32), pltpu.VMEM((tq, 2 * tq), F32),
                        pltpu.VMEM((1, 2 * tq), F32), pltpu.VMEM((1, 2 * tq), F32),
                        pltpu.VMEM((2, LANES), F32)],
    )
    return pl.pallas_call(
        functools.partial(_attn_kernel, nk=nq, tk=tq, tq=tq),
        grid_spec=grid_spec,
        out_shape=jax.ShapeDtypeStruct((batch * seq, ATTN_W), BF16),
        compiler_params=_cparams(("arbitrary", "arbitrary", "arbitrary")), name="diff_attn",
    )(slopes, qT, k, vT, diff_lambda.astype(F32), g_col)


def _rec_mats(reverse):
    c = REC_CHUNK
    t = np.arange(c)[:, None]
    u = np.arange(c)[None, :]
    mats = [(u >= t) if reverse else (u <= t)]
    for lv in REC_LEVELS:
        if reverse:
            bd = (t // (2 * lv)) * (2 * lv) + lv
            mats.append(u >= bd)
        else:
            bd = (t // (2 * lv)) * (2 * lv) + lv - 1
            mats.append(u <= bd)
    return jnp.asarray(np.concatenate(mats, axis=0).astype(np.float32), dtype=BF16)


def _rec_gates(q_ref, z_ref, v_ref, lbp_ref, mats_ref, n_chunks):
    c = REC_CHUNK
    lbp = lbp_ref[...]
    mx = jnp.maximum(lbp[0:1], lbp[1:2])
    e0 = jnp.exp(lbp[0:1] - mx)
    e1 = jnp.exp(lbp[1:2] - mx)
    lb = e0 / (e0 + e1)

    rows = [slice(n * c, (n + 1) * c) for n in range(n_chunks)]
    qs = [q_ref[sl, :] for sl in rows]
    vs = [v_ref[sl, :].astype(BF16) for sl in rows]
    gs, kks = [], []
    for sl in rows:
        z = z_ref[sl, :]
        sig = 1.0 / (1.0 + jnp.exp(-z))
        gs.append(jnp.log(lb + (1.0 - lb) * sig))
        kks.append((1.0 - lb) * (1.0 / (1.0 + jnp.exp(z))))

    g_cat = jnp.concatenate(gs, axis=1)
    g1 = g_cat.astype(BF16)
    r1 = g_cat - g1.astype(F32)
    g2 = r1.astype(BF16)
    g3 = (r1 - g2.astype(F32)).astype(BF16)
    mats = mats_ref[...]
    stacked = (_dot(mats, g3) + _dot(mats, g2)) + _dot(mats, g1)
    return qs, kks, vs, stacked


def _rec_scores(qs, kks, vs, stacked, reverse):
    c = REC_CHUNK
    n_chunks = len(qs)
    row = lax.broadcasted_iota(I32, (c, HEAD_W), 0)
    ti = lax.broadcasted_iota(I32, (c, c), 0)
    si = lax.broadcasted_iota(I32, (c, c), 1)
    is_query, same = [], []
    for lv in REC_LEVELS:
        in_upper = (row & (2 * lv - 1)) >= lv
        is_query.append(jnp.logical_not(in_upper) if reverse else in_upper)
        shift = (2 * lv).bit_length() - 1
        same.append((ti >> shift) == (si >> shift))
    diag = ti == si

    o_intra, incs, qes, decays = [], [], [], []
    for n in range(n_chunks):
        lanes = slice(n * HEAD_W, (n + 1) * HEAD_W)
        b = stacked[0:c, lanes]
        q, kk = qs[n], kks[n]
        a = jnp.where(diag, _dot_nt(q.astype(BF16), kk.astype(BF16)), 0.0)
        for m, lv in enumerate(REC_LEVELS):
            r = stacked[(m + 1) * c:(m + 2) * c, lanes]
            e = jnp.exp(-jnp.abs(b - r))
            qt = jnp.where(is_query[m], q * e, 0.0).astype(BF16)
            kt = jnp.where(is_query[m], 0.0, kk * e).astype(BF16)
            a = a + jnp.where(same[m], _dot_nt(qt, kt), 0.0)
        b_end = b[0:1] if reverse else b[c - 1:c]
        o_intra.append(_dot(a.astype(BF16), vs[n]))
        incs.append(_dot_tn(vs[n], (kk * jnp.exp(b_end - b)).astype(BF16)))
        qes.append((q * jnp.exp(b)).astype(BF16))
        decays.append(jnp.exp(b_end))
    return o_intra, incs, qes, decays


def _rec_scan(o_intra, incs, qes, decays, o_ref, st_ref, reverse):
    n_chunks = len(o_intra)
    st = st_ref[...]
    order = range(n_chunks - 1, -1, -1) if reverse else range(n_chunks)
    for n in order:
        o_ref[n * REC_CHUNK:(n + 1) * REC_CHUNK, :] = o_intra[n] + _dot_nt(qes[n], st.astype(BF16))
        st = st * decays[n] + incs[n]
    st_ref[...] = st


def _rec_kernel(qf_ref, zf_ref, vf_ref, qb_ref, zb_ref, vb_ref, lbpf_ref, lbpb_ref, matsf_ref,
                matsb_ref, of_ref, ob_ref, stf_ref, stb_ref, *, n_chunks):
    @pl.when(pl.program_id(2) == 0)
    def _():
        stf_ref[...] = jnp.zeros(stf_ref.shape, F32)
        stb_ref[...] = jnp.zeros(stb_ref.shape, F32)

    gates_f = _rec_gates(qf_ref, zf_ref, vf_ref, lbpf_ref, matsf_ref, n_chunks)
    gates_b = _rec_gates(qb_ref, zb_ref, vb_ref, lbpb_ref, matsb_ref, n_chunks)
    parts_f = _rec_scores(*gates_f, False)
    parts_b = _rec_scores(*gates_b, True)
    _rec_scan(*parts_f, of_ref, stf_ref, False)
    _rec_scan(*parts_b, ob_ref, stb_ref, True)


def _recurrence(rec, rec_lower_bound, batch, seq, tb):
    nb = seq // tb
    lbp = rec_lower_bound.astype(F32)

    def fwd(col):
        return lambda b, h, i: (b * nb + i, col + h)

    def bwd(col):
        return lambda b, h, i: (b * nb + nb - 1 - i, col + h)

    blk = (tb, HEAD_W)
    mat_shape = ((len(REC_LEVELS) + 1) * REC_CHUNK, REC_CHUNK)
    out = jax.ShapeDtypeStruct((batch * seq, REC_W), F32)
    return pl.pallas_call(
        functools.partial(_rec_kernel, n_chunks=tb // REC_CHUNK),
        grid=(batch, HEADS, nb),
        in_specs=[
            pl.BlockSpec(blk, fwd(0)), pl.BlockSpec(blk, fwd(HEADS)), pl.BlockSpec(blk, fwd(3 * HEADS)),
            pl.BlockSpec(blk, bwd(0)), pl.BlockSpec(blk, bwd(2 * HEADS)), pl.BlockSpec(blk, bwd(3 * HEADS)),
            pl.BlockSpec((2, HEAD_W), lambda b, h, i: (0, h)),
            pl.BlockSpec((2, HEAD_W), lambda b, h, i: (0, h)),
            pl.BlockSpec(mat_shape, lambda b, h, i: (0, 0)),
            pl.BlockSpec(mat_shape, lambda b, h, i: (0, 0)),
        ],
        out_specs=[pl.BlockSpec(blk, fwd(0)), pl.BlockSpec(blk, bwd(0))],
        out_shape=[out, out],
        scratch_shapes=[pltpu.VMEM((HEAD_W, HEAD_W), F32), pltpu.VMEM((HEAD_W, HEAD_W), F32)],
        compiler_params=_cparams(("parallel", "parallel", "arbitrary")), name="rec_bidir",
    )(rec, rec, rec, rec, rec, rec, lbp[0], lbp[1], _rec_mats(False), _rec_mats(True))


def _split_bf16(x):
    hi = x.astype(BF16)
    lo = (x - hi.astype(F32)).astype(BF16)
    return hi, lo


def _outproj_kernel(x_ref, ao_ref, of_ref, ob_ref, rg_ref, rng_ref, woa_ref, wor_ref,
                    fg_ref, wrh_ref, wrl_ref, x1_ref, x1acc_ref, aff_ref):
    parts = []
    for hd in range(HEADS):
        sl = slice(hd * HEAD_W, (hd + 1) * HEAD_W)
        o = of_ref[:, sl] + ob_ref[:, sl]
        parts.append((_rms(o, rng_ref[:, sl]) * rg_ref[:, sl]).astype(BF16))
    ro = jnp.concatenate(parts, axis=-1)
    x1 = x_ref[...] + _dot(ao_ref[...], woa_ref[...]) + _dot(ro, wor_ref[...])
    x1_ref[...] = x1
    x1acc_ref[...] = x1
    h2 = _rms(x1, fg_ref[...])
    hi, lo = _split_bf16(h2)
    logits = _dot(hi, wrh_ref[...]) + _dot(lo, wrh_ref[...]) + _dot(hi, wrl_ref[...])
    lane = lax.broadcasted_iota(I32, logits.shape, 1)
    logits = jnp.where(lane < N_EXPERTS, logits, -jnp.inf)
    mx = jnp.max(logits, axis=-1, keepdims=True)
    ex = jnp.exp(logits - mx)
    aff = ex / jnp.sum(ex, axis=-1, keepdims=True)
    aff_ref[...] = aff.T[:N_EXPERTS]


def _outproj(x2d, ao, o_f, o_b, rec, rec_norm_g, w_out, norm_ffn_g, w_router, tm):
    n_tok = x2d.shape[0]
    w_bf = w_out.astype(BF16)
    wr = jnp.pad(w_router.astype(F32), ((0, 0), (0, LANES - N_EXPERTS)))
    wr_hi = wr.astype(BF16)
    wr_lo = (wr - wr_hi.astype(F32)).astype(BF16)
    row = lambda i: (i, 0)
    fixed = lambda i: (0, 0)
    return pl.pallas_call(
        _outproj_kernel,
        grid=(n_tok // tm,),
        in_specs=[
            pl.BlockSpec((tm, D_MODEL), row),
            pl.BlockSpec((tm, ATTN_W), row),
            pl.BlockSpec((tm, REC_W), row),
            pl.BlockSpec((tm, REC_W), row),
            pl.BlockSpec((tm, REC_W), lambda i: (i, 4)),
            pl.BlockSpec((1, REC_W), fixed),
            pl.BlockSpec((ATTN_W, D_MODEL), fixed),
            pl.BlockSpec((REC_W, D_MODEL), fixed),
            pl.BlockSpec((1, D_MODEL), fixed),
            pl.BlockSpec((D_MODEL, LANES), fixed),
            pl.BlockSpec((D_MODEL, LANES), fixed),
        ],
        out_specs=[pl.BlockSpec((tm, D_MODEL), row), pl.BlockSpec((tm, D_MODEL), row),
                   pl.BlockSpec((N_EXPERTS, tm), lambda i: (0, i))],
        out_shape=[jax.ShapeDtypeStruct((n_tok, D_MODEL), F32),
                   jax.ShapeDtypeStruct((n_tok, D_MODEL), F32),
                   jax.ShapeDtypeStruct((N_EXPERTS, n_tok), F32)],
        compiler_params=_cparams(("parallel",)), name="outproj_router",
    )(x2d, ao, o_f, o_b, rec, rec_norm_g.astype(F32).reshape(1, REC_W),
      w_bf[:ATTN_W], w_bf[ATTN_W:], norm_ffn_g.astype(F32).reshape(1, D_MODEL), wr_hi, wr_lo)


TOPK_BLK = 256


def _topk_kernel(aff_ref, tri_ref, idx_ref, gate_ref, *, n_tok, cap):
    aff = aff_ref[...]

    def search(step, cand):
        trial = cand | (jnp.int32(1) << (30 - step))
        n_ge = jnp.sum((aff >= pltpu.bitcast(trial, F32)).astype(I32), axis=1, keepdims=True)
        return jnp.where(n_ge >= cap, trial, cand)

    thr = pltpu.bitcast(lax.fori_loop(0, 31, search, jnp.zeros((N_EXPERTS, 1), I32)), F32)
    gt = aff > thr
    tie = aff == thr
    need = (cap - jnp.sum(gt.astype(I32), axis=1, keepdims=True)).astype(F32)

    def running_count(flags):
        f = flags.astype(F32).astype(BF16)
        carry = jnp.zeros((N_EXPERTS, 1), F32)
        out = []
        for j in range(n_tok // TOPK_BLK):
            pre = _dot(f[:, j * TOPK_BLK:(j + 1) * TOPK_BLK], tri_ref[...]) + carry
            out.append(pre)
            carry = pre[:, TOPK_BLK - 1:TOPK_BLK]
        return jnp.concatenate(out, axis=1)

    sel = jnp.logical_or(gt, jnp.logical_and(tie, running_count(tie) <= need))
    slot = running_count(sel).astype(I32) - 1

    lane = lax.broadcasted_iota(I32, (N_EXPERTS, n_tok), 1)
    packed = jnp.where(sel, (slot << 16) | lane, -1)
    val = aff
    for k in range((n_tok - 1).bit_length()):
        step = 1 << k
        dist = lane - (packed >> 16)
        moving = jnp.logical_and(packed >= 0, ((dist >> k) & 1) == 1)
        arriving = pltpu.roll(jnp.where(moving, packed, -1), n_tok - step, axis=1)
        arriving_val = pltpu.roll(val, n_tok - step, axis=1)
        lands = arriving >= 0
        packed = jnp.where(lands, arriving, jnp.where(moving, -1, packed))
        val = jnp.where(lands, arriving_val, val)
    idx_ref[...] = packed[:, :cap] & 0xFFFF
    gate_ref[...] = val[:, :cap]


def _topk(aff_t, cap):
    n_tok = aff_t.shape[1]
    assert n_tok <= 32768 and n_tok % TOPK_BLK == 0, "token index and slot share one int32"
    r = np.arange(TOPK_BLK)
    tri = jnp.asarray((r[:, None] <= r[None, :]).astype(np.float32), dtype=BF16)
    return pl.pallas_call(
        functools.partial(_topk_kernel, n_tok=n_tok, cap=cap),
        out_shape=[jax.ShapeDtypeStruct((N_EXPERTS, cap), I32),
                   jax.ShapeDtypeStruct((N_EXPERTS, cap), F32)],
        compiler_params=pltpu.CompilerParams(vmem_limit_bytes=VMEM_LIMIT), name="expert_topk",
    )(aff_t, tri)


ROW_DMA_UNROLL = 8


def _row_copy(hbm, vmem, t, p, sem, gather):
    if gather:
        return pltpu.make_async_copy(hbm.at[pl.ds(t, 1), :], vmem.at[pl.ds(p, 1), :], sem)
    return pltpu.make_async_copy(vmem.at[pl.ds(p, 1), :], hbm.at[pl.ds(t, 1), :], sem)


def _rows_start_loop(hbm, vmem, idx_ref, base, cap, sem, gather):
    def issue(blk, carry):
        for u in range(ROW_DMA_UNROLL):
            p = blk * ROW_DMA_UNROLL + u
            _row_copy(hbm, vmem, idx_ref[base + p], p, sem, gather).start()
        return carry
    lax.fori_loop(0, cap // ROW_DMA_UNROLL, issue, 0)


def _rows_start_inline(hbm, vmem, idx_ref, base, p0, count, sem, gather):
    for u in range(count):
        _row_copy(hbm, vmem, idx_ref[base + p0 + u], p0 + u, sem, gather).start()


def _rows_wait(hbm, vmem, cap, sem, gather):
    if gather:
        pltpu.make_async_copy(hbm.at[pl.ds(0, cap), :], vmem, sem).wait()
    else:
        pltpu.make_async_copy(vmem, hbm.at[pl.ds(0, cap), :], sem).wait()


def _moe_kernel(idx_ref, gate_ref, fg_ref, x_hbm, acc_in_hbm, wg_ref, wu_ref, wd_ref, out_hbm,
                xg, rows, xe, ye, sem_x, sem_r, sem_s, *, cap, nf):
    del acc_in_hbm
    e = pl.program_id(0)
    f = pl.program_id(1)
    last_e = pl.num_programs(0) - 1
    chunk = cap // nf

    @pl.when(f == 0)
    def _():
        @pl.when(e == 0)
        def _():
            _rows_start_loop(x_hbm, xg, idx_ref, 0, cap, sem_x, True)
            _rows_wait(x_hbm, xg, cap, sem_x, True)

        @pl.when(e > 0)
        def _():
            _rows_wait(out_hbm, rows, cap, sem_s, False)

        xe[...] = _rms(xg[...], fg_ref[...]).astype(BF16)
        ye[...] = jnp.zeros(ye.shape, F32)

    _rows_start_inline(x_hbm, xg, idx_ref, jnp.minimum(e + 1, last_e) * cap, f * chunk, chunk,
                       sem_x, True)
    _rows_start_inline(out_hbm, rows, idx_ref, e * cap, f * chunk, chunk, sem_r, True)

    xb = xe[...]
    hg = _dot(xb, wg_ref[0].astype(BF16))
    hu = _dot(xb, wu_ref[0].astype(BF16))
    hid = hg * (1.0 / (1.0 + jnp.exp(-hg))) * hu
    ye[...] += _dot(hid.astype(BF16), wd_ref[0].astype(BF16))

    @pl.when(f == nf - 1)
    def _():
        _rows_wait(x_hbm, xg, cap, sem_x, True)
        _rows_wait(out_hbm, rows, cap, sem_r, True)
        rows[...] = rows[...] + ye[...] * gate_ref[0]
        _rows_start_inline(out_hbm, rows, idx_ref, e * cap, 0, cap, sem_s, False)

        @pl.when(e == last_e)
        def _():
            _rows_wait(out_hbm, rows, cap, sem_s, False)


def _moe(x1, x1_acc, idx, gates, norm_ffn_g, w_gate, w_up, w_down, tf):
    n_tok = x1.shape[0]
    cap = idx.shape[1]
    nf = D_EXPERT // tf
    grid_spec = pltpu.PrefetchScalarGridSpec(
        num_scalar_prefetch=1,
        grid=(N_EXPERTS, nf),
        in_specs=[
            pl.BlockSpec((1, cap, 1), lambda e, f, s: (e, 0, 0)),
            pl.BlockSpec((1, D_MODEL), lambda e, f, s: (0, 0)),
            pl.BlockSpec(memory_space=pl.ANY),
            pl.BlockSpec(memory_space=pl.ANY),
            pl.BlockSpec((1, D_MODEL, tf), lambda e, f, s: (e, 0, f)),
            pl.BlockSpec((1, D_MODEL, tf), lambda e, f, s: (e, 0, f)),
            pl.BlockSpec((1, tf, D_MODEL), lambda e, f, s: (e, f, 0)),
        ],
        out_specs=pl.BlockSpec(memory_space=pl.ANY),
        scratch_shapes=[pltpu.VMEM((cap, D_MODEL), F32), pltpu.VMEM((cap, D_MODEL), F32),
                        pltpu.VMEM((cap, D_MODEL), BF16), pltpu.VMEM((cap, D_MODEL), F32),
                        pltpu.SemaphoreType.DMA(()), pltpu.SemaphoreType.DMA(()),
                        pltpu.SemaphoreType.DMA(())],
    )
    return pl.pallas_call(
        functools.partial(_moe_kernel, cap=cap, nf=nf),
        grid_spec=grid_spec,
        out_shape=jax.ShapeDtypeStruct((n_tok, D_MODEL), F32),
        input_output_aliases={4: 0},
        compiler_params=_cparams(("arbitrary", "arbitrary")), name="expert_ffn",
    )(idx.reshape(N_EXPERTS * cap), gates.reshape(N_EXPERTS, cap, 1),
      norm_ffn_g.astype(F32).reshape(1, D_MODEL),
      x1, x1_acc, w_gate, w_up, w_down)


def _ple_kernel(x_ref, p_ref, g_ref, wg_ref, wp_ref, o_ref):
    x = x_ref[...]
    zg = _dot(_rms(x, g_ref[...]).astype(BF16), wg_ref[...])
    gate = 1.0 / (1.0 + jnp.exp(-zg))
    o_ref[...] = x + gate * _dot(p_ref[...].astype(BF16), wp_ref[...])


def _ple(x2, p2d, norm_ple_g, w_ple_gate, w_ple_proj, tm):
    n_tok = x2.shape[0]
    row = lambda i: (i, 0)
    fixed = lambda i: (0, 0)
    return pl.pallas_call(
        _ple_kernel,
        grid=(n_tok // tm,),
        in_specs=[pl.BlockSpec((tm, D_MODEL), row), pl.BlockSpec((tm, PLE_DIM), row),
                  pl.BlockSpec((1, D_MODEL), fixed), pl.BlockSpec((D_MODEL, D_MODEL), fixed),
                  pl.BlockSpec((PLE_DIM, D_MODEL), fixed)],
        out_specs=pl.BlockSpec((tm, D_MODEL), row),
        out_shape=jax.ShapeDtypeStruct((n_tok, D_MODEL), F32),
        compiler_params=_cparams(("parallel",)), name="ple_gate",
    )(x2, p2d, norm_ple_g.astype(F32).reshape(1, D_MODEL), w_ple_gate.astype(BF16),
      w_ple_proj.astype(BF16))


def _tiles(batch, seq):
    t_attn = min(512, seq)
    n_blk = batch * seq // t_attn
    t_proj = t_attn * (2 if n_blk % 2 == 0 else 1)
    t_rec = min(512, seq)
    t_row = min(256, seq)
    return t_attn, t_proj, t_rec, t_row


def _layer(x, p, norm_mix_g, w_in, q_norm_g, k_norm_g, diff_lambda, diff_norm_g, rec_lower_bound,
           rec_norm_g, w_out, norm_ffn_g, w_router, w_expert_gate, w_expert_up, w_expert_down,
           norm_ple_g, w_ple_gate, w_ple_proj):
    batch, seq, _ = x.shape
    n_tok = batch * seq
    t_attn, t_proj, t_rec, t_row = _tiles(batch, seq)
    x2d = x.reshape(n_tok, D_MODEL)
    qT, k, vT, rec = _projections(x2d, norm_mix_g[0], w_in[0], q_norm_g[0], k_norm_g[0], t_proj,
                                  t_attn)
    ao = _attention(qT, k, vT, diff_lambda[0], diff_norm_g[0], batch, seq, t_attn)
    o_f, o_b = _recurrence(rec, rec_lower_bound, batch, seq, t_rec)
    x1, x1_acc, aff = _outproj(x2d, ao, o_f, o_b, rec, rec_norm_g[0], w_out[0], norm_ffn_g[0],
                               w_router[0], t_row)
    cap = max(1, (CAPACITY_FACTOR * n_tok) // N_EXPERTS)
    idx, gates = _topk(aff, cap)
    x2 = _moe(x1, x1_acc, idx, gates, norm_ffn_g[0], w_expert_gate[0], w_expert_up[0],
              w_expert_down[0], tf=256)
    y = _ple(x2, p[0].reshape(n_tok, PLE_DIM), norm_ple_g[0], w_ple_gate[0], w_ple_proj[0], t_row)
    return y.reshape(x.shape)


def kernel(x_prompt, x_sample, p_prompt, p_sample, norm_mix_g, w_in, q_norm_g, k_norm_g, diff_lambda, diff_norm_g, rec_lower_bound, rec_norm_g, w_out, norm_ffn_g, w_router, w_expert_gate, w_expert_up, w_expert_down, norm_ple_g, w_ple_gate, w_ple_proj):
    weights = (norm_mix_g, w_in, q_norm_g, k_norm_g, diff_lambda, diff_norm_g, rec_lower_bound,
               rec_norm_g, w_out, norm_ffn_g, w_router, w_expert_gate, w_expert_up, w_expert_down,
               norm_ple_g, w_ple_gate, w_ple_proj)
    return (_layer(x_prompt, p_prompt, *weights), _layer(x_sample, p_sample, *weights))
```

```python
import functools
import math

import numpy as np
import jax
import jax.numpy as jnp
from jax import lax
from jax.experimental import pallas as pl
from jax.experimental.pallas import tpu as pltpu

F32 = jnp.float32
BF16 = jnp.bfloat16
I32 = jnp.int32

D_MODEL = 2048
PLE_DIM = 256
HEADS = 8
QK_DIM = 64
V_DIM = 128
HEAD_W = 128
ATTN_W = HEADS * V_DIM
REC_W = HEADS * V_DIM
N_EXPERTS = 16
CAPACITY_FACTOR = 2
D_EXPERT = 2048
NORM_EPS = 1e-6
LAM_INIT = 0.8 - 0.6 * math.exp(-0.3 * 0)

LANES = 128
LOG2E = 1.4426950408889634
VMEM_LIMIT = 56 * 1024 * 1024

REC_CHUNK = 64
REC_LEVELS = (1, 2, 4, 8, 16, 32)


def _cparams(sem, vmem=VMEM_LIMIT):
    return pltpu.CompilerParams(dimension_semantics=sem, vmem_limit_bytes=vmem)


def _dot(a, b):
    return jnp.dot(a, b, preferred_element_type=F32)


def _dot_nt(a, b):
    return lax.dot_general(a, b, (((1,), (1,)), ((), ())), preferred_element_type=F32)


def _dot_tn(a, b):
    return lax.dot_general(a, b, (((0,), (0,)), ((), ())), preferred_element_type=F32)


def _rms(x, g):
    ms = jnp.mean(x * x, axis=-1, keepdims=True)
    return x * lax.rsqrt(ms + NORM_EPS) * g


PROJ_TN = 512
QK_GROUP_TILE = 256
Q_BLKS = ATTN_W // PROJ_TN
REC_SEGS = 5


def _proj_kernel(x_ref, g_ref, w_ref, qg_ref, kg_ref, bd_ref, qT_ref, k_ref, vT_ref, rec_ref, h_ref,
                 *, tq):
    j = pl.program_id(1)

    @pl.when(j == 0)
    def _():
        h_ref[...] = _rms(x_ref[...], g_ref[...]).astype(BF16)

    tm = h_ref.shape[0]

    def project():
        return _dot(h_ref[...], w_ref[...])

    def qk_norm(gain_row):
        acc = project()
        parts = []
        for c0 in range(0, PROJ_TN, QK_GROUP_TILE):
            sub = acc[:, c0:c0 + QK_GROUP_TILE]
            ss = _dot((sub * sub).astype(BF16), bd_ref[...])
            parts.append(sub * lax.rsqrt(ss * (1.0 / QK_DIM) + NORM_EPS))
        return jnp.concatenate(parts, axis=1) * gain_row

    def store_transposed(o_ref, val):
        for r in range(tm // tq):
            o_ref[r] = val[r * tq:(r + 1) * tq, :].T.astype(o_ref.dtype)

    @pl.when(j < Q_BLKS)
    def _():
        store_transposed(qT_ref, qk_norm(qg_ref[...]))

    @pl.when(jnp.logical_and(j >= Q_BLKS, j < 2 * Q_BLKS))
    def _():
        k_ref[...] = qk_norm(kg_ref[...]).astype(k_ref.dtype)

    @pl.when(jnp.logical_and(j >= 2 * Q_BLKS, j < 3 * Q_BLKS))
    def _():
        store_transposed(vT_ref, project())

    @pl.when(j >= 3 * Q_BLKS)
    def _():
        acc = project()
        seg = (j - 3 * Q_BLKS) // Q_BLKS
        is_silu = jnp.logical_or(seg == 0, seg == REC_SEGS - 1)
        rec_ref[...] = jnp.where(is_silu, acc * (1.0 / (1.0 + jnp.exp(-acc))), acc)


def _projections(x2d, norm_g, w_in, q_norm_g, k_norm_g, tm, tq):
    n_tok = x2d.shape[0]
    qg_row = jnp.tile(q_norm_g.astype(F32) * (QK_DIM ** -0.5 * LOG2E), 2 * HEADS).reshape(1, ATTN_W)
    kg_row = jnp.tile(k_norm_g.astype(F32), 2 * HEADS).reshape(1, ATTN_W)
    grp = np.arange(QK_GROUP_TILE) // QK_DIM
    bd = jnp.asarray((grp[:, None] == grp[None, :]).astype(np.float32), dtype=BF16)
    n_col_blocks = (3 * ATTN_W + REC_SEGS * REC_W) // PROJ_TN

    def seg_block(first):
        def index(j, n_blocks):
            return jnp.clip(j - first, 0, n_blocks - 1)
        return index

    q_blk, k_blk, v_blk, r_blk = (seg_block(0), seg_block(Q_BLKS), seg_block(2 * Q_BLKS),
                                  seg_block(3 * Q_BLKS))
    t_shape = (tm // tq, PROJ_TN, tq)
    return pl.pallas_call(
        functools.partial(_proj_kernel, tq=tq),
        grid=(n_tok // tm, n_col_blocks),
        in_specs=[
            pl.BlockSpec((tm, D_MODEL), lambda i, j: (i, 0)),
            pl.BlockSpec((1, D_MODEL), lambda i, j: (0, 0)),
            pl.BlockSpec((D_MODEL, PROJ_TN), lambda i, j: (0, j)),
            pl.BlockSpec((1, PROJ_TN), lambda i, j: (0, q_blk(j, Q_BLKS))),
            pl.BlockSpec((1, PROJ_TN), lambda i, j: (0, k_blk(j, Q_BLKS))),
            pl.BlockSpec((QK_GROUP_TILE, QK_GROUP_TILE), lambda i, j: (0, 0)),
        ],
        out_specs=[
            pl.BlockSpec(t_shape, lambda i, j: (i, q_blk(j, Q_BLKS), 0)),
            pl.BlockSpec((tm, PROJ_TN), lambda i, j: (i, k_blk(j, Q_BLKS))),
            pl.BlockSpec(t_shape, lambda i, j: (i, v_blk(j, Q_BLKS), 0)),
            pl.BlockSpec((tm, PROJ_TN), lambda i, j: (i, r_blk(j, REC_SEGS * Q_BLKS))),
        ],
        out_shape=[
            jax.ShapeDtypeStruct((n_tok // tq, ATTN_W, tq), BF16),
            jax.ShapeDtypeStruct((n_tok, ATTN_W), BF16),
            jax.ShapeDtypeStruct((n_tok // tq, ATTN_W, tq), BF16),
            jax.ShapeDtypeStruct((n_tok, REC_SEGS * REC_W), F32),
        ],
        scratch_shapes=[pltpu.VMEM((tm, D_MODEL), BF16)],
        compiler_params=_cparams(("parallel", "arbitrary")), name="in_proj",
    )(x2d, norm_g.reshape(1, D_MODEL), w_in.astype(BF16), qg_row, kg_row, bd)


L_ROWS = 16
SKIP_LOG2_MARGIN = 160.0
SKIP_NORM_SLACK = 1.02


def _attn_kernel(slopes_ref, qT_ref, k_ref, vT_ref, lam_ref, g_ref, o_ref,
                 m_ref, acc_ref, bias_ref, ta_ref, tb_ref, mta_ref, mtb_ref, kn_ref, *, nk, tk, tq):
    h = pl.program_id(1)
    i = pl.program_id(2)
    sl = slopes_ref[h]
    lane_h = lax.broadcasted_iota(I32, (1, HEAD_W), 1)
    lane_v = lax.broadcasted_iota(I32, (1, LANES), 1)

    @pl.when(i == 0)
    def _():
        rel = (lax.broadcasted_iota(I32, (tk, tq), 1)
               - lax.broadcasted_iota(I32, (tk, tq), 0)).astype(F32)
        sr = sl * rel
        sa = sl * jnp.abs(rel)
        bias_ref[0] = jnp.concatenate([sr, sr], axis=1)
        bias_ref[1] = jnp.concatenate([sa, sa], axis=1)
        bias_ref[2] = jnp.concatenate([-sr, -sr], axis=1)

        def block_norm(j, carry):
            kf = k_ref[pl.ds(pl.multiple_of(j * tk, tk), tk), :].astype(F32)
            sq = kf * kf
            out = []
            for c in range(2):
                in_map = (lane_h >= c * QK_DIM) & (lane_h < (c + 1) * QK_DIM)
                n2 = jnp.sum(jnp.where(in_map, sq, 0.0), axis=1, keepdims=True)
                nmax = jnp.sqrt(jnp.max(n2, axis=0, keepdims=True))
                out.append(jnp.where(lane_v == j, nmax, carry[c]))
            return tuple(out)

        kn0, kn1 = lax.fori_loop(0, nk, block_norm,
                                 (jnp.zeros((1, LANES), F32), jnp.zeros((1, LANES), F32)))
        kn_ref[0:1, :] = kn0
        kn_ref[1:2, :] = kn1

    qT = qT_ref[0]

    qf = qT.astype(F32)
    qsq = qf * qf
    thr = None
    for c in range(2):
        rows_c = slice(c * QK_DIM, (c + 1) * QK_DIM)
        bq = jnp.sqrt(jnp.max(jnp.sum(qsq[rows_c], axis=0, keepdims=True), axis=1, keepdims=True))
        kn = kn_ref[c:c + 1, :]
        k_all = jnp.max(kn, axis=1, keepdims=True)
        k_diag = jnp.max(jnp.where(lane_v == i, kn, 0.0), axis=1, keepdims=True)
        thr_c = SKIP_NORM_SLACK * bq * (k_all + k_diag) + SKIP_LOG2_MARGIN
        thr = thr_c if thr is None else jnp.maximum(thr, thr_c)
    dist = lane_v.astype(F32)
    visit = (lane_v >= 1) & (sl * ((dist - 1.0) * tq + 1.0) <= thr)
    radius = jnp.sum(visit.astype(I32))
    lo = jnp.maximum(i - radius, 0)
    hi = jnp.minimum(i + radius, nk - 1)
    n_vis = hi - lo + 1
    zero_half = jnp.zeros((QK_DIM, tq), qT.dtype)
    q_cat = jnp.concatenate([jnp.concatenate([qT[:QK_DIM], zero_half], axis=0),
                             jnp.concatenate([zero_half, qT[QK_DIM:]], axis=0)], axis=1)
    ones = jnp.ones((L_ROWS, tk), BF16)
    m_ref[...] = jnp.full(m_ref.shape, -jnp.inf, F32)
    acc_ref[...] = jnp.zeros(acc_ref.shape, F32)

    def shift_of(j):
        gap = jnp.abs(jnp.full((1, 2 * tq), (i - j) * tq, I32)).astype(F32)
        return -sl * gap

    def scores(j, t_ref, mt_ref):
        kblk = k_ref[pl.ds(pl.multiple_of(j * tk, tk), tk), :]
        side = jnp.where(j < i, 0, jnp.where(j == i, 1, 2))
        t = _dot(kblk, q_cat) - bias_ref[side]
        t_ref[...] = t
        mt_ref[...] = jnp.max(t, axis=0, keepdims=True) + shift_of(j)

    def accumulate(j, t_ref, mt_ref):
        m_old = m_ref[...]
        m_new = jnp.maximum(m_old, mt_ref[...])
        p = jnp.exp2(t_ref[...] - (m_new - shift_of(j)))
        v_aug = jnp.concatenate([vT_ref[j], ones], axis=0)
        acc_ref[...] = jnp.exp2(m_old - m_new) * acc_ref[...] + _dot(v_aug, p.astype(BF16))
        m_ref[...] = m_new

    def pair(jj, carry):
        j = lo + 2 * jj
        scores(j + 1, tb_ref, mtb_ref)
        accumulate(j, ta_ref, mta_ref)
        scores(j + 2, ta_ref, mta_ref)
        accumulate(j + 1, tb_ref, mtb_ref)
        return carry

    scores(lo, ta_ref, mta_ref)
    lax.fori_loop(0, jnp.right_shift(n_vis - 1, 1), pair, 0)

    @pl.when(jnp.bitwise_and(n_vis, 1) == 1)
    def _():
        accumulate(hi, ta_ref, mta_ref)

    @pl.when(jnp.bitwise_and(n_vis, 1) == 0)
    def _():
        scores(hi, tb_ref, mtb_ref)
        accumulate(hi - 1, ta_ref, mta_ref)
        accumulate(hi, tb_ref, mtb_ref)

    lp = lam_ref[...]
    lam = (jnp.exp(jnp.sum(lp[0:1] * lp[1:2], axis=-1, keepdims=True))
           - jnp.exp(jnp.sum(lp[2:3] * lp[3:4], axis=-1, keepdims=True)) + LAM_INIT)
    acc = acc_ref[...]
    a0 = acc[:, :tq]
    a1 = acc[:, tq:]
    o = (a0[:V_DIM] / a0[V_DIM:V_DIM + 1]
         - lam * (a1[:V_DIM] / a1[V_DIM:V_DIM + 1]))
    ms = jnp.mean(o * o, axis=0, keepdims=True)
    on = o * lax.rsqrt(ms + NORM_EPS) * g_ref[...] * (1.0 - LAM_INIT)
    o_ref[...] = on.T.astype(o_ref.dtype)


def _attention(qT, k, vT, diff_lambda, diff_norm_g, batch, seq, tq):
    nq = seq // tq
    assert nq <= LANES, "per-block key norms are kept one per lane"
    slopes = LOG2E * jnp.exp2(-8.0 * jnp.arange(1, HEADS + 1, dtype=F32) / HEADS)
    g_col = diff_norm_g.astype(F32).reshape(ATTN_W, 1)
    grid_spec = pltpu.PrefetchScalarGridSpec(
        num_scalar_prefetch=1,
        grid=(batch, HEADS, nq),
        in_specs=[
            pl.BlockSpec((1, HEAD_W, tq), lambda b, h, i, s: (b * nq + i, h, 0)),
            pl.BlockSpec((seq, HEAD_W), lambda b, h, i, s: (b, h)),
            pl.BlockSpec((nq, HEAD_W, tq), lambda b, h, i, s: (b, h, 0)),
            pl.BlockSpec((4, QK_DIM), lambda b, h, i, s: (0, 0)),
            pl.BlockSpec((HEAD_W, 1), lambda b, h, i, s: (h, 0)),
        ],
        out_specs=pl.BlockSpec((tq, HEAD_W), lambda b, h, i, s: (b * nq + i, h)),
        scratch_shapes=[pltpu.VMEM((1, 2 * tq), F32),
                        pltpu.VMEM((V_DIM + L_ROWS, 2 * tq), F32),
                        pltpu.VMEM((3, tq, 2 * tq), F32),
                        pltpu.VMEM((tq, 2 * tq), F32), pltpu.VMEM((tq, 2 * tq), F32),
                        pltpu.VMEM((1, 2 * tq), F32), pltpu.VMEM((1, 2 * tq), F32),
                        pltpu.VMEM((2, LANES), F32)],
    )
    return pl.pallas_call(
        functools.partial(_attn_kernel, nk=nq, tk=tq, tq=tq),
        grid_spec=grid_spec,
        out_shape=jax.ShapeDtypeStruct((batch * seq, ATTN_W), BF16),
        compiler_params=_cparams(("arbitrary", "arbitrary", "arbitrary")), name="diff_attn",
    )(slopes, qT, k, vT, diff_lambda.astype(F32), g_col)


def _rec_mats(reverse):
    c = REC_CHUNK
    t = np.arange(c)[:, None]
    u = np.arange(c)[None, :]
    mats = [(u >= t) if reverse else (u <= t)]
    for lv in REC_LEVELS:
        if reverse:
            bd = (t // (2 * lv)) * (2 * lv) + lv
            mats.append(u >= bd)
        else:
            bd = (t // (2 * lv)) * (2 * lv) + lv - 1
            mats.append(u <= bd)
    return jnp.asarray(np.concatenate(mats, axis=0).astype(np.float32), dtype=BF16)


def _rec_gates(q_ref, z_ref, v_ref, lbp_ref, mats_ref, n_chunks):
    c = REC_CHUNK
    lbp = lbp_ref[...]
    mx = jnp.maximum(lbp[0:1], lbp[1:2])
    e0 = jnp.exp(lbp[0:1] - mx)
    e1 = jnp.exp(lbp[1:2] - mx)
    lb = e0 / (e0 + e1)

    rows = [slice(n * c, (n + 1) * c) for n in range(n_chunks)]
    qs = [q_ref[sl, :] for sl in rows]
    vs = [v_ref[sl, :].astype(BF16) for sl in rows]
    gs, kks = [], []
    for sl in rows:
        z = z_ref[sl, :]
        sig = 1.0 / (1.0 + jnp.exp(-z))
        gs.append(jnp.log(lb + (1.0 - lb) * sig))
        kks.append((1.0 - lb) * (1.0 / (1.0 + jnp.exp(z))))

    g_cat = jnp.concatenate(gs, axis=1)
    g1 = g_cat.astype(BF16)
    r1 = g_cat - g1.astype(F32)
    g2 = r1.astype(BF16)
    g3 = (r1 - g2.astype(F32)).astype(BF16)
    mats = mats_ref[...]
    stacked = (_dot(mats, g3) + _dot(mats, g2)) + _dot(mats, g1)
    return qs, kks, vs, stacked


def _rec_scores(qs, kks, vs, stacked, reverse):
    c = REC_CHUNK
    n_chunks = len(qs)
    row = lax.broadcasted_iota(I32, (c, HEAD_W), 0)
    ti = lax.broadcasted_iota(I32, (c, c), 0)
    si = lax.broadcasted_iota(I32, (c, c), 1)
    is_query, same = [], []
    for lv in REC_LEVELS:
        in_upper = (row & (2 * lv - 1)) >= lv
        is_query.append(jnp.logical_not(in_upper) if reverse else in_upper)
        shift = (2 * lv).bit_length() - 1
        same.append((ti >> shift) == (si >> shift))
    diag = ti == si

    o_intra, incs, qes, decays = [], [], [], []
    for n in range(n_chunks):
        lanes = slice(n * HEAD_W, (n + 1) * HEAD_W)
        b = stacked[0:c, lanes]
        q, kk = qs[n], kks[n]
        a = jnp.where(diag, _dot_nt(q.astype(BF16), kk.astype(BF16)), 0.0)
        for m, lv in enumerate(REC_LEVELS):
            r = stacked[(m + 1) * c:(m + 2) * c, lanes]
            e = jnp.exp(-jnp.abs(b - r))
            qt = jnp.where(is_query[m], q * e, 0.0).astype(BF16)
            kt = jnp.where(is_query[m], 0.0, kk * e).astype(BF16)
            a = a + jnp.where(same[m], _dot_nt(qt, kt), 0.0)
        b_end = b[0:1] if reverse else b[c - 1:c]
        o_intra.append(_dot(a.astype(BF16), vs[n]))
        incs.append(_dot_tn(vs[n], (kk * jnp.exp(b_end - b)).astype(BF16)))
        qes.append((q * jnp.exp(b)).astype(BF16))
        decays.append(jnp.exp(b_end))
    return o_intra, incs, qes, decays


def _rec_scan(o_intra, incs, qes, decays, o_ref, st_ref, reverse):
    n_chunks = len(o_intra)
    st = st_ref[...]
    order = range(n_chunks - 1, -1, -1) if reverse else range(n_chunks)
    for n in order:
        o_ref[n * REC_CHUNK:(n + 1) * REC_CHUNK, :] = o_intra[n] + _dot_nt(qes[n], st.astype(BF16))
        st = st * decays[n] + incs[n]
    st_ref[...] = st


def _rec_kernel(qf_ref, zf_ref, vf_ref, qb_ref, zb_ref, vb_ref, lbpf_ref, lbpb_ref, matsf_ref,
                matsb_ref, of_ref, ob_ref, stf_ref, stb_ref, *, n_chunks):
    @pl.when(pl.program_id(2) == 0)
    def _():
        stf_ref[...] = jnp.zeros(stf_ref.shape, F32)
        stb_ref[...] = jnp.zeros(stb_ref.shape, F32)

    gates_f = _rec_gates(qf_ref, zf_ref, vf_ref, lbpf_ref, matsf_ref, n_chunks)
    gates_b = _rec_gates(qb_ref, zb_ref, vb_ref, lbpb_ref, matsb_ref, n_chunks)
    parts_f = _rec_scores(*gates_f, False)
    parts_b = _rec_scores(*gates_b, True)
    _rec_scan(*parts_f, of_ref, stf_ref, False)
    _rec_scan(*parts_b, ob_ref, stb_ref, True)


def _recurrence(rec, rec_lower_bound, batch, seq, tb):
    nb = seq // tb
    lbp = rec_lower_bound.astype(F32)

    def fwd(col):
        return lambda b, h, i: (b * nb + i, col + h)

    def bwd(col):
        return lambda b, h, i: (b * nb + nb - 1 - i, col + h)

    blk = (tb, HEAD_W)
    mat_shape = ((len(REC_LEVELS) + 1) * REC_CHUNK, REC_CHUNK)
    out = jax.ShapeDtypeStruct((batch * seq, REC_W), F32)
    return pl.pallas_call(
        functools.partial(_rec_kernel, n_chunks=tb // REC_CHUNK),
        grid=(batch, HEADS, nb),
        in_specs=[
            pl.BlockSpec(blk, fwd(0)), pl.BlockSpec(blk, fwd(HEADS)), pl.BlockSpec(blk, fwd(3 * HEADS)),
            pl.BlockSpec(blk, bwd(0)), pl.BlockSpec(blk, bwd(2 * HEADS)), pl.BlockSpec(blk, bwd(3 * HEADS)),
            pl.BlockSpec((2, HEAD_W), lambda b, h, i: (0, h)),
            pl.BlockSpec((2, HEAD_W), lambda b, h, i: (0, h)),
            pl.BlockSpec(mat_shape, lambda b, h, i: (0, 0)),
            pl.BlockSpec(mat_shape, lambda b, h, i: (0, 0)),
        ],
        out_specs=[pl.BlockSpec(blk, fwd(0)), pl.BlockSpec(blk, bwd(0))],
        out_shape=[out, out],
        scratch_shapes=[pltpu.VMEM((HEAD_W, HEAD_W), F32), pltpu.VMEM((HEAD_W, HEAD_W), F32)],
        compiler_params=_cparams(("parallel", "parallel", "arbitrary")), name="rec_bidir",
    )(rec, rec, rec, rec, rec, rec, lbp[0], lbp[1], _rec_mats(False), _rec_mats(True))


def _split_bf16(x):
    hi = x.astype(BF16)
    lo = (x - hi.astype(F32)).astype(BF16)
    return hi, lo


def _outproj_kernel(x_ref, ao_ref, of_ref, ob_ref, rg_ref, rng_ref, woa_ref, wor_ref,
                    fg_ref, wrh_ref, wrl_ref, x1_ref, x1acc_ref, aff_ref):
    parts = []
    for hd in range(HEADS):
        sl = slice(hd * HEAD_W, (hd + 1) * HEAD_W)
        o = of_ref[:, sl] + ob_ref[:, sl]
        parts.append((_rms(o, rng_ref[:, sl]) * rg_ref[:, sl]).astype(BF16))
    ro = jnp.concatenate(parts, axis=-1)
    x1 = x_ref[...] + _dot(ao_ref[...], woa_ref[...]) + _dot(ro, wor_ref[...])
    x1_ref[...] = x1
    x1acc_ref[...] = x1
    h2 = _rms(x1, fg_ref[...])
    hi, lo = _split_bf16(h2)
    logits = _dot(hi, wrh_ref[...]) + _dot(lo, wrh_ref[...]) + _dot(hi, wrl_ref[...])
    lane = lax.broadcasted_iota(I32, logits.shape, 1)
    logits = jnp.where(lane < N_EXPERTS, logits, -jnp.inf)
    mx = jnp.max(logits, axis=-1, keepdims=True)
    ex = jnp.exp(logits - mx)
    aff = ex / jnp.sum(ex, axis=-1, keepdims=True)
    aff_ref[...] = aff.T[:N_EXPERTS]


def _outproj(x2d, ao, o_f, o_b, rec, rec_norm_g, w_out, norm_ffn_g, w_router, tm):
    n_tok = x2d.shape[0]
    w_bf = w_out.astype(BF16)
    wr = jnp.pad(w_router.astype(F32), ((0, 0), (0, LANES - N_EXPERTS)))
    wr_hi = wr.astype(BF16)
    wr_lo = (wr - wr_hi.astype(F32)).astype(BF16)
    row = lambda i: (i, 0)
    fixed = lambda i: (0, 0)
    return pl.pallas_call(
        _outproj_kernel,
        grid=(n_tok // tm,),
        in_specs=[
            pl.BlockSpec((tm, D_MODEL), row),
            pl.BlockSpec((tm, ATTN_W), row),
            pl.BlockSpec((tm, REC_W), row),
            pl.BlockSpec((tm, REC_W), row),
            pl.BlockSpec((tm, REC_W), lambda i: (i, 4)),
            pl.BlockSpec((1, REC_W), fixed),
            pl.BlockSpec((ATTN_W, D_MODEL), fixed),
            pl.BlockSpec((REC_W, D_MODEL), fixed),
            pl.BlockSpec((1, D_MODEL), fixed),
            pl.BlockSpec((D_MODEL, LANES), fixed),
            pl.BlockSpec((D_MODEL, LANES), fixed),
        ],
        out_specs=[pl.BlockSpec((tm, D_MODEL), row), pl.BlockSpec((tm, D_MODEL), row),
                   pl.BlockSpec((N_EXPERTS, tm), lambda i: (0, i))],
        out_shape=[jax.ShapeDtypeStruct((n_tok, D_MODEL), F32),
                   jax.ShapeDtypeStruct((n_tok, D_MODEL), F32),
                   jax.ShapeDtypeStruct((N_EXPERTS, n_tok), F32)],
        compiler_params=_cparams(("parallel",)), name="outproj_router",
    )(x2d, ao, o_f, o_b, rec, rec_norm_g.astype(F32).reshape(1, REC_W),
      w_bf[:ATTN_W], w_bf[ATTN_W:], norm_ffn_g.astype(F32).reshape(1, D_MODEL), wr_hi, wr_lo)


TOPK_BLK = 256


def _topk_kernel(aff_ref, tri_ref, idx_ref, gate_ref, *, n_tok, cap):
    aff = aff_ref[...]

    def search(step, cand):
        trial = cand | (jnp.int32(1) << (30 - step))
        n_ge = jnp.sum((aff >= pltpu.bitcast(trial, F32)).astype(I32), axis=1, keepdims=True)
        return jnp.where(n_ge >= cap, trial, cand)

    thr = pltpu.bitcast(lax.fori_loop(0, 31, search, jnp.zeros((N_EXPERTS, 1), I32)), F32)
    gt = aff > thr
    tie = aff == thr
    need = (cap - jnp.sum(gt.astype(I32), axis=1, keepdims=True)).astype(F32)

    def running_count(flags):
        f = flags.astype(F32).astype(BF16)
        carry = jnp.zeros((N_EXPERTS, 1), F32)
        out = []
        for j in range(n_tok // TOPK_BLK):
            pre = _dot(f[:, j * TOPK_BLK:(j + 1) * TOPK_BLK], tri_ref[...]) + carry
            out.append(pre)
            carry = pre[:, TOPK_BLK - 1:TOPK_BLK]
        return jnp.concatenate(out, axis=1)

    sel = jnp.logical_or(gt, jnp.logical_and(tie, running_count(tie) <= need))
    slot = running_count(sel).astype(I32) - 1

    lane = lax.broadcasted_iota(I32, (N_EXPERTS, n_tok), 1)
    packed = jnp.where(sel, (slot << 16) | lane, -1)
    val = aff
    for k in range((n_tok - 1).bit_length()):
        step = 1 << k
        dist = lane - (packed >> 16)
        moving = jnp.logical_and(packed >= 0, ((dist >> k) & 1) == 1)
        arriving = pltpu.roll(jnp.where(moving, packed, -1), n_tok - step, axis=1)
        arriving_val = pltpu.roll(val, n_tok - step, axis=1)
        lands = arriving >= 0
        packed = jnp.where(lands, arriving, jnp.where(moving, -1, packed))
        val = jnp.where(lands, arriving_val, val)
    idx_ref[...] = packed[:, :cap] & 0xFFFF
    gate_ref[...] = val[:, :cap]


def _topk(aff_t, cap):
    n_tok = aff_t.shape[1]
    assert n_tok <= 32768 and n_tok % TOPK_BLK == 0, "token index and slot share one int32"
    r = np.arange(TOPK_BLK)
    tri = jnp.asarray((r[:, None] <= r[None, :]).astype(np.float32), dtype=BF16)
    return pl.pallas_call(
        functools.partial(_topk_kernel, n_tok=n_tok, cap=cap),
        out_shape=[jax.ShapeDtypeStruct((N_EXPERTS, cap), I32),
                   jax.ShapeDtypeStruct((N_EXPERTS, cap), F32)],
        compiler_params=pltpu.CompilerParams(vmem_limit_bytes=VMEM_LIMIT), name="expert_topk",
    )(aff_t, tri)


ROW_DMA_UNROLL = 8


def _row_copy(hbm, vmem, t, p, sem, gather):
    if gather:
        return pltpu.make_async_copy(hbm.at[pl.ds(t, 1), :], vmem.at[pl.ds(p, 1), :], sem)
    return pltpu.make_async_copy(vmem.at[pl.ds(p, 1), :], hbm.at[pl.ds(t, 1), :], sem)


def _rows_start_loop(hbm, vmem, idx_ref, base, cap, sem, gather):
    def issue(blk, carry):
        for u in range(ROW_DMA_UNROLL):
            p = blk * ROW_DMA_UNROLL + u
            _row_copy(hbm, vmem, idx_ref[base + p], p, sem, gather).start()
        return carry
    lax.fori_loop(0, cap // ROW_DMA_UNROLL, issue, 0)


def _rows_start_inline(hbm, vmem, idx_ref, base, p0, count, sem, gather):
    for u in range(count):
        _row_copy(hbm, vmem, idx_ref[base + p0 + u], p0 + u, sem, gather).start()


def _rows_wait(hbm, vmem, cap, sem, gather):
    if gather:
        pltpu.make_async_copy(hbm.at[pl.ds(0, cap), :], vmem, sem).wait()
    else:
        pltpu.make_async_copy(vmem, hbm.at[pl.ds(0, cap), :], sem).wait()


def _moe_kernel(idx_ref, gate_ref, fg_ref, x_hbm, acc_in_hbm, wg_ref, wu_ref, wd_ref, out_hbm,
                xg, rows, xe, ye, sem_x, sem_r, sem_s, *, cap, nf):
    del acc_in_hbm
    e = pl.program_id(0)
    f = pl.program_id(1)
    last_e = pl.num_programs(0) - 1
    chunk = cap // nf

    @pl.when(f == 0)
    def _():
        @pl.when(e == 0)
        def _():
            _rows_start_loop(x_hbm, xg, idx_ref, 0, cap, sem_x, True)
            _rows_wait(x_hbm, xg, cap, sem_x, True)

        @pl.when(e > 0)
        def _():
            _rows_wait(out_hbm, rows, cap, sem_s, False)

        xe[...] = _rms(xg[...], fg_ref[...]).astype(BF16)
        ye[...] = jnp.zeros(ye.shape, F32)

    _rows_start_inline(x_hbm, xg, idx_ref, jnp.minimum(e + 1, last_e) * cap, f * chunk, chunk,
                       sem_x, True)
    _rows_start_inline(out_hbm, rows, idx_ref, e * cap, f * chunk, chunk, sem_r, True)

    xb = xe[...]
    hg = _dot(xb, wg_ref[0].astype(BF16))
    hu = _dot(xb, wu_ref[0].astype(BF16))
    hid = hg * (1.0 / (1.0 + jnp.exp(-hg))) * hu
    ye[...] += _dot(hid.astype(BF16), wd_ref[0].astype(BF16))

    @pl.when(f == nf - 1)
    def _():
        _rows_wait(x_hbm, xg, cap, sem_x, True)
        _rows_wait(out_hbm, rows, cap, sem_r, True)
        rows[...] = rows[...] + ye[...] * gate_ref[0]
        _rows_start_inline(out_hbm, rows, idx_ref, e * cap, 0, cap, sem_s, False)

        @pl.when(e == last_e)
        def _():
            _rows_wait(out_hbm, rows, cap, sem_s, False)


def _moe(x1, x1_acc, idx, gates, norm_ffn_g, w_gate, w_up, w_down, tf):
    n_tok = x1.shape[0]
    cap = idx.shape[1]
    nf = D_EXPERT // tf
    grid_spec = pltpu.PrefetchScalarGridSpec(
        num_scalar_prefetch=1,
        grid=(N_EXPERTS, nf),
        in_specs=[
            pl.BlockSpec((1, cap, 1), lambda e, f, s: (e, 0, 0)),
            pl.BlockSpec((1, D_MODEL), lambda e, f, s: (0, 0)),
            pl.BlockSpec(memory_space=pl.ANY),
            pl.BlockSpec(memory_space=pl.ANY),
            pl.BlockSpec((1, D_MODEL, tf), lambda e, f, s: (e, 0, f)),
            pl.BlockSpec((1, D_MODEL, tf), lambda e, f, s: (e, 0, f)),
            pl.BlockSpec((1, tf, D_MODEL), lambda e, f, s: (e, f, 0)),
        ],
        out_specs=pl.BlockSpec(memory_space=pl.ANY),
        scratch_shapes=[pltpu.VMEM((cap, D_MODEL), F32), pltpu.VMEM((cap, D_MODEL), F32),
                        pltpu.VMEM((cap, D_MODEL), BF16), pltpu.VMEM((cap, D_MODEL), F32),
                        pltpu.SemaphoreType.DMA(()), pltpu.SemaphoreType.DMA(()),
                        pltpu.SemaphoreType.DMA(())],
    )
    return pl.pallas_call(
        functools.partial(_moe_kernel, cap=cap, nf=nf),
        grid_spec=grid_spec,
        out_shape=jax.ShapeDtypeStruct((n_tok, D_MODEL), F32),
        input_output_aliases={4: 0},
        compiler_params=_cparams(("arbitrary", "arbitrary")), name="expert_ffn",
    )(idx.reshape(N_EXPERTS * cap), gates.reshape(N_EXPERTS, cap, 1),
      norm_ffn_g.astype(F32).reshape(1, D_MODEL),
      x1, x1_acc, w_gate, w_up, w_down)


def _ple_kernel(x_ref, p_ref, g_ref, wg_ref, wp_ref, o_ref):
    x = x_ref[...]
    zg = _dot(_rms(x, g_ref[...]).astype(BF16), wg_ref[...])
    gate = 1.0 / (1.0 + jnp.exp(-zg))
    o_ref[...] = x + gate * _dot(p_ref[...].astype(BF16), wp_ref[...])


def _ple(x2, p2d, norm_ple_g, w_ple_gate, w_ple_proj, tm):
    n_tok = x2.shape[0]
    row = lambda i: (i, 0)
    fixed = lambda i: (0, 0)
    return pl.pallas_call(
        _ple_kernel,
        grid=(n_tok // tm,),
        in_specs=[pl.BlockSpec((tm, D_MODEL), row), pl.BlockSpec((tm, PLE_DIM), row),
                  pl.BlockSpec((1, D_MODEL), fixed), pl.BlockSpec((D_MODEL, D_MODEL), fixed),
                  pl.BlockSpec((PLE_DIM, D_MODEL), fixed)],
        out_specs=pl.BlockSpec((tm, D_MODEL), row),
        out_shape=jax.ShapeDtypeStruct((n_tok, D_MODEL), F32),
        compiler_params=_cparams(("parallel",)), name="ple_gate",
    )(x2, p2d, norm_ple_g.astype(F32).reshape(1, D_MODEL), w_ple_gate.astype(BF16),
      w_ple_proj.astype(BF16))


def _tiles(batch, seq):
    t_attn = min(512, seq)
    n_blk = batch * seq // t_attn
    t_proj = t_attn * (2 if n_blk % 2 == 0 else 1)
    t_rec = min(512, seq)
    t_row = min(256, seq)
    return t_attn, t_proj, t_rec, t_row


def _layer(x, p, norm_mix_g, w_in, q_norm_g, k_norm_g, diff_lambda, diff_norm_g, rec_lower_bound,
           rec_norm_g, w_out, norm_ffn_g, w_router, w_expert_gate, w_expert_up, w_expert_down,
           norm_ple_g, w_ple_gate, w_ple_proj):
    batch, seq, _ = x.shape
    n_tok = batch * seq
    t_attn, t_proj, t_rec, t_row = _tiles(batch, seq)
    x2d = x.reshape(n_tok, D_MODEL)
    qT, k, vT, rec = _projections(x2d, norm_mix_g[0], w_in[0], q_norm_g[0], k_norm_g[0], t_proj,
                                  t_attn)
    ao = _attention(qT, k, vT, diff_lambda[0], diff_norm_g[0], batch, seq, t_attn)
    o_f, o_b = _recurrence(rec, rec_lower_bound, batch, seq, t_rec)
    x1, x1_acc, aff = _outproj(x2d, ao, o_f, o_b, rec, rec_norm_g[0], w_out[0], norm_ffn_g[0],
                               w_router[0], t_row)
    cap = max(1, (CAPACITY_FACTOR * n_tok) // N_EXPERTS)
    idx, gates = _topk(aff, cap)
    x2 = _moe(x1, x1_acc, idx, gates, norm_ffn_g[0], w_expert_gate[0], w_expert_up[0],
              w_expert_down[0], tf=256)
    y = _ple(x2, p[0].reshape(n_tok, PLE_DIM), norm_ple_g[0], w_ple_gate[0], w_ple_proj[0], t_row)
    return y.reshape(x.shape)


def kernel(x_prompt, x_sample, p_prompt, p_sample, norm_mix_g, w_in, q_norm_g, k_norm_g, diff_lambda, diff_norm_g, rec_lower_bound, rec_norm_g, w_out, norm_ffn_g, w_router, w_expert_gate, w_expert_up, w_expert_down, norm_ple_g, w_ple_gate, w_ple_proj):
    weights = (norm_mix_g, w_in, q_norm_g, k_norm_g, diff_lambda, diff_norm_g, rec_lower_bound,
               rec_norm_g, w_out, norm_ffn_g, w_router, w_expert_gate, w_expert_up, w_expert_down,
               norm_ple_g, w_ple_gate, w_ple_proj)
    return (_layer(x_prompt, p_prompt, *weights), _layer(x_sample, p_sample, *weights))
```

```python
import functools
import math

import numpy as np
import jax
import jax.numpy as jnp
from jax import lax
from jax.experimental import pallas as pl
from jax.experimental.pallas import tpu as pltpu

F32 = jnp.float32
BF16 = jnp.bfloat16
I32 = jnp.int32

D_MODEL = 2048
PLE_DIM = 256
HEADS = 8
QK_DIM = 64
V_DIM = 128
HEAD_W = 128
ATTN_W = HEADS * V_DIM
REC_W = HEADS * V_DIM
N_EXPERTS = 16
CAPACITY_FACTOR = 2
D_EXPERT = 2048
NORM_EPS = 1e-6
LAM_INIT = 0.8 - 0.6 * math.exp(-0.3 * 0)

LANES = 128
LOG2E = 1.4426950408889634
VMEM_LIMIT = 56 * 1024 * 1024

REC_CHUNK = 64
REC_LEVELS = (1, 2, 4, 8, 16, 32)


def _cparams(sem, vmem=VMEM_LIMIT):
    return pltpu.CompilerParams(dimension_semantics=sem, vmem_limit_bytes=vmem)


def _dot(a, b):
    return jnp.dot(a, b, preferred_element_type=F32)


def _dot_nt(a, b):
    return lax.dot_general(a, b, (((1,), (1,)), ((), ())), preferred_element_type=F32)


def _dot_tn(a, b):
    return lax.dot_general(a, b, (((0,), (0,)), ((), ())), preferred_element_type=F32)


def _rms(x, g):
    ms = jnp.mean(x * x, axis=-1, keepdims=True)
    return x * lax.rsqrt(ms + NORM_EPS) * g


PROJ_TN = 512
QK_GROUP_TILE = 256
Q_BLKS = ATTN_W // PROJ_TN
REC_SEGS = 5


def _proj_kernel(x_ref, g_ref, w_ref, qg_ref, kg_ref, bd_ref, qT_ref, k_ref, vT_ref, rec_ref, h_ref,
                 *, tq):
    j = pl.program_id(1)

    @pl.when(j == 0)
    def _():
        h_ref[...] = _rms(x_ref[...], g_ref[...]).astype(BF16)

    tm = h_ref.shape[0]

    def project():
        return _dot(h_ref[...], w_ref[...])

    def qk_norm(gain_row):
        acc = project()
        parts = []
        for c0 in range(0, PROJ_TN, QK_GROUP_TILE):
            sub = acc[:, c0:c0 + QK_GROUP_TILE]
            ss = _dot((sub * sub).astype(BF16), bd_ref[...])
            parts.append(sub * lax.rsqrt(ss * (1.0 / QK_DIM) + NORM_EPS))
        return jnp.concatenate(parts, axis=1) * gain_row

    def store_transposed(o_ref, val):
        for r in range(tm // tq):
            o_ref[r] = val[r * tq:(r + 1) * tq, :].T.astype(o_ref.dtype)

    @pl.when(j < Q_BLKS)
    def _():
        store_transposed(qT_ref, qk_norm(qg_ref[...]))

    @pl.when(jnp.logical_and(j >= Q_BLKS, j < 2 * Q_BLKS))
    def _():
        k_ref[...] = qk_norm(kg_ref[...]).astype(k_ref.dtype)

    @pl.when(jnp.logical_and(j >= 2 * Q_BLKS, j < 3 * Q_BLKS))
    def _():
        store_transposed(vT_ref, project())

    @pl.when(j >= 3 * Q_BLKS)
    def _():
        acc = project()
        seg = (j - 3 * Q_BLKS) // Q_BLKS
        is_silu = jnp.logical_or(seg == 0, seg == REC_SEGS - 1)
        rec_ref[...] = jnp.where(is_silu, acc * (1.0 / (1.0 + jnp.exp(-acc))), acc)


def _projections(x2d, norm_g, w_in, q_norm_g, k_norm_g, tm, tq):
    n_tok = x2d.shape[0]
    qg_row = jnp.tile(q_norm_g.astype(F32) * (QK_DIM ** -0.5 * LOG2E), 2 * HEADS).reshape(1, ATTN_W)
    kg_row = jnp.tile(k_norm_g.astype(F32), 2 * HEADS).reshape(1, ATTN_W)
    grp = np.arange(QK_GROUP_TILE) // QK_DIM
    bd = jnp.asarray((grp[:, None] == grp[None, :]).astype(np.float32), dtype=BF16)
    n_col_blocks = (3 * ATTN_W + REC_SEGS * REC_W) // PROJ_TN

    def seg_block(first):
        def index(j, n_blocks):
            return jnp.clip(j - first, 0, n_blocks - 1)
        return index

    q_blk, k_blk, v_blk, r_blk = (seg_block(0), seg_block(Q_BLKS), seg_block(2 * Q_BLKS),
                                  seg_block(3 * Q_BLKS))
    t_shape = (tm // tq, PROJ_TN, tq)
    return pl.pallas_call(
        functools.partial(_proj_kernel, tq=tq),
        grid=(n_tok // tm, n_col_blocks),
        in_specs=[
            pl.BlockSpec((tm, D_MODEL), lambda i, j: (i, 0)),
            pl.BlockSpec((1, D_MODEL), lambda i, j: (0, 0)),
            pl.BlockSpec((D_MODEL, PROJ_TN), lambda i, j: (0, j)),
            pl.BlockSpec((1, PROJ_TN), lambda i, j: (0, q_blk(j, Q_BLKS))),
            pl.BlockSpec((1, PROJ_TN), lambda i, j: (0, k_blk(j, Q_BLKS))),
            pl.BlockSpec((QK_GROUP_TILE, QK_GROUP_TILE), lambda i, j: (0, 0)),
        ],
        out_specs=[
            pl.BlockSpec(t_shape, lambda i, j: (i, q_blk(j, Q_BLKS), 0)),
            pl.BlockSpec((tm, PROJ_TN), lambda i, j: (i, k_blk(j, Q_BLKS))),
            pl.BlockSpec(t_shape, lambda i, j: (i, v_blk(j, Q_BLKS), 0)),
            pl.BlockSpec((tm, PROJ_TN), lambda i, j: (i, r_blk(j, REC_SEGS * Q_BLKS))),
        ],
        out_shape=[
            jax.ShapeDtypeStruct((n_tok // tq, ATTN_W, tq), BF16),
            jax.ShapeDtypeStruct((n_tok, ATTN_W), BF16),
            jax.ShapeDtypeStruct((n_tok // tq, ATTN_W, tq), BF16),
            jax.ShapeDtypeStruct((n_tok, REC_SEGS * REC_W), F32),
        ],
        scratch_shapes=[pltpu.VMEM((tm, D_MODEL), BF16)],
        compiler_params=_cparams(("parallel", "arbitrary")), name="in_proj",
    )(x2d, norm_g.reshape(1, D_MODEL), w_in.astype(BF16), qg_row, kg_row, bd)


L_ROWS = 16
SKIP_LOG2_MARGIN = 160.0
SKIP_NORM_SLACK = 1.02


def _attn_kernel(slopes_ref, qT_ref, k_ref, vT_ref, lam_ref, g_ref, o_ref,
                 m_ref, acc_ref, bias_ref, ta_ref, tb_ref, mta_ref, mtb_ref, kn_ref, *, nk, tk, tq):
    h = pl.program_id(1)
    i = pl.program_id(2)
    sl = slopes_ref[h]
    lane_h = lax.broadcasted_iota(I32, (1, HEAD_W), 1)
    lane_v = lax.broadcasted_iota(I32, (1, LANES), 1)

    @pl.when(i == 0)
    def _():
        rel = (lax.broadcasted_iota(I32, (tk, tq), 1)
               - lax.broadcasted_iota(I32, (tk, tq), 0)).astype(F32)
        sr = sl * rel
        sa = sl * jnp.abs(rel)
        bias_ref[0] = jnp.concatenate([sr, sr], axis=1)
        bias_ref[1] = jnp.concatenate([sa, sa], axis=1)
        bias_ref[2] = jnp.concatenate([-sr, -sr], axis=1)

        def block_norm(j, carry):
            kf = k_ref[pl.ds(pl.multiple_of(j * tk, tk), tk), :].astype(F32)
            sq = kf * kf
            out = []
            for c in range(2):
                in_map = (lane_h >= c * QK_DIM) & (lane_h < (c + 1) * QK_DIM)
                n2 = jnp.sum(jnp.where(in_map, sq, 0.0), axis=1, keepdims=True)
                nmax = jnp.sqrt(jnp.max(n2, axis=0, keepdims=True))
                out.append(jnp.where(lane_v == j, nmax, carry[c]))
            return tuple(out)

        kn0, kn1 = lax.fori_loop(0, nk, block_norm,
                                 (jnp.zeros((1, LANES), F32), jnp.zeros((1, LANES), F32)))
        kn_ref[0:1, :] = kn0
        kn_ref[1:2, :] = kn1

    qT = qT_ref[0]

    qf = qT.astype(F32)
    qsq = qf * qf
    thr = None
    for c in range(2):
        rows_c = slice(c * QK_DIM, (c + 1) * QK_DIM)
        bq = jnp.sqrt(jnp.max(jnp.sum(qsq[rows_c], axis=0, keepdims=True), axis=1, keepdims=True))
        kn = kn_ref[c:c + 1, :]
        k_all = jnp.max(kn, axis=1, keepdims=True)
        k_diag = jnp.max(jnp.where(lane_v == i, kn, 0.0), axis=1, keepdims=True)
        thr_c = SKIP_NORM_SLACK * bq * (k_all + k_diag) + SKIP_LOG2_MARGIN
        thr = thr_c if thr is None else jnp.maximum(thr, thr_c)
    dist = lane_v.astype(F32)
    visit = (lane_v >= 1) & (sl * ((dist - 1.0) * tq + 1.0) <= thr)
    radius = jnp.sum(visit.astype(I32))
    lo = jnp.maximum(i - radius, 0)
    hi = jnp.minimum(i + radius, nk - 1)
    n_vis = hi - lo + 1
    zero_half = jnp.zeros((QK_DIM, tq), qT.dtype)
    q_cat = jnp.concatenate([jnp.concatenate([qT[:QK_DIM], zero_half], axis=0),
                             jnp.concatenate([zero_half, qT[QK_DIM:]], axis=0)], axis=1)
    ones = jnp.ones((L_ROWS, tk), BF16)
    m_ref[...] = jnp.full(m_ref.shape, -jnp.inf, F32)
    acc_ref[...] = jnp.zeros(acc_ref.shape, F32)

    def shift_of(j):
        gap = jnp.abs(jnp.full((1, 2 * tq), (i - j) * tq, I32)).astype(F32)
        return -sl * gap

    def scores(j, t_ref, mt_ref):
        kblk = k_ref[pl.ds(pl.multiple_of(j * tk, tk), tk), :]
        side = jnp.where(j < i, 0, jnp.where(j == i, 1, 2))
        t = _dot(kblk, q_cat) - bias_ref[side]
        t_ref[...] = t
        mt_ref[...] = jnp.max(t, axis=0, keepdims=True) + shift_of(j)

    def accumulate(j, t_ref, mt_ref):
        m_old = m_ref[...]
        m_new = jnp.maximum(m_old, mt_ref[...])
        p = jnp.exp2(t_ref[...] - (m_new - shift_of(j)))
        v_aug = jnp.concatenate([vT_ref[j], ones], axis=0)
        acc_ref[...] = jnp.exp2(m_old - m_new) * acc_ref[...] + _dot(v_aug, p.astype(BF16))
        m_ref[...] = m_new

    def pair(jj, carry):
        j = lo + 2 * jj
        scores(j + 1, tb_ref, mtb_ref)
        accumulate(j, ta_ref, mta_ref)
        scores(j + 2, ta_ref, mta_ref)
        accumulate(j + 1, tb_ref, mtb_ref)
        return carry

    scores(lo, ta_ref, mta_ref)
    lax.fori_loop(0, jnp.right_shift(n_vis - 1, 1), pair, 0)

    @pl.when(jnp.bitwise_and(n_vis, 1) == 1)
    def _():
        accumulate(hi, ta_ref, mta_ref)

    @pl.when(jnp.bitwise_and(n_vis, 1) == 0)
    def _():
        scores(hi, tb_ref, mtb_ref)
        accumulate(hi - 1, ta_ref, mta_ref)
        accumulate(hi, tb_ref, mtb_ref)

    lp = lam_ref[...]
    lam = (jnp.exp(jnp.sum(lp[0:1] * lp[1:2], axis=-1, keepdims=True))
           - jnp.exp(jnp.sum(lp[2:3] * lp[3:4], axis=-1, keepdims=True)) + LAM_INIT)
    acc = acc_ref[...]
    a0 = acc[:, :tq]
    a1 = acc[:, tq:]
    o = (a0[:V_DIM] / a0[V_DIM:V_DIM + 1]
         - lam * (a1[:V_DIM] / a1[V_DIM:V_DIM + 1]))
    ms = jnp.mean(o * o, axis=0, keepdims=True)
    on = o * lax.rsqrt(ms + NORM_EPS) * g_ref[...] * (1.0 - LAM_INIT)
    o_ref[...] = on.T.astype(o_ref.dtype)


def _attention(qT, k, vT, diff_lambda, diff_norm_g, batch, seq, tq):
    nq = seq // tq
    assert nq <= LANES, "per-block key norms are kept one per lane"
    slopes = LOG2E * jnp.exp2(-8.0 * jnp.arange(1, HEADS + 1, dtype=F32) / HEADS)
    g_col = diff_norm_g.astype(F32).reshape(ATTN_W, 1)
    grid_spec = pltpu.PrefetchScalarGridSpec(
        num_scalar_prefetch=1,
        grid=(batch, HEADS, nq),
        in_specs=[
            pl.BlockSpec((1, HEAD_W, tq), lambda b, h, i, s: (b * nq + i, h, 0)),
            pl.BlockSpec((seq, HEAD_W), lambda b, h, i, s: (b, h)),
            pl.BlockSpec((nq, HEAD_W, tq), lambda b, h, i, s: (b, h, 0)),
            pl.BlockSpec((4, QK_DIM), lambda b, h, i, s: (0, 0)),
            pl.BlockSpec((HEAD_W, 1), lambda b, h, i, s: (h, 0)),
        ],
        out_specs=pl.BlockSpec((tq, HEAD_W), lambda b, h, i, s: (b * nq + i, h)),
        scratch_shapes=[pltpu.VMEM((1, 2 * tq), F32),
                        pltpu.VMEM((V_DIM + L_ROWS, 2 * tq), F32),
                        pltpu.VMEM((3, tq, 2 * tq), F32),
                        pltpu.VMEM((tq, 2 * tq), F32), pltpu.VMEM((tq, 2 * tq), F32),
                        pltpu.VMEM((1, 2 * tq), F32), pltpu.VMEM((1, 2 * tq), F32),
                        pltpu.VMEM((2, LANES), F32)],
    )
    return pl.pallas_call(
        functools.partial(_attn_kernel, nk=nq, tk=tq, tq=tq),
        grid_spec=grid_spec,
        out_shape=jax.ShapeDtypeStruct((batch * seq, ATTN_W), BF16),
        compiler_params=_cparams(("arbitrary", "arbitrary", "arbitrary")), name="diff_attn",
    )(slopes, qT, k, vT, diff_lambda.astype(F32), g_col)


SUBLANES = 8
REC_MAT_LEVELS = tuple(lv for lv in REC_LEVELS if 4 * lv < SUBLANES)


def _rec_boundary_row(lv, group, reverse):
    return group * 2 * lv + (lv if reverse else lv - 1)


def _rec_mats(reverse):
    c = REC_CHUNK
    t = np.arange(c)[:, None]
    u = np.arange(c)[None, :]
    mats = [(u >= t) if reverse else (u <= t)]
    for lv in REC_MAT_LEVELS:
        bd = _rec_boundary_row(lv, t // (2 * lv), reverse)
        mats.append((u >= bd) if reverse else (u <= bd))
    return jnp.asarray(np.concatenate(mats, axis=0).astype(np.float32), dtype=BF16)


def _rec_gates(q_ref, z_ref, v_ref, lbp_ref, mats_ref, n_chunks):
    c = REC_CHUNK
    lbp = lbp_ref[...]
    mx = jnp.maximum(lbp[0:1], lbp[1:2])
    e0 = jnp.exp(lbp[0:1] - mx)
    e1 = jnp.exp(lbp[1:2] - mx)
    lb = e0 / (e0 + e1)

    rows = [slice(n * c, (n + 1) * c) for n in range(n_chunks)]
    qs = [q_ref[sl, :] for sl in rows]
    vs = [v_ref[sl, :].astype(BF16) for sl in rows]
    gs, kks = [], []
    for sl in rows:
        z = z_ref[sl, :]
        sig = 1.0 / (1.0 + jnp.exp(-z))
        gs.append(jnp.log2(lb + (1.0 - lb) * sig))
        kks.append((1.0 - lb) * (1.0 / (1.0 + jnp.exp(z))))

    g_cat = jnp.concatenate(gs, axis=1)
    g1 = g_cat.astype(BF16)
    r1 = g_cat - g1.astype(F32)
    g2 = r1.astype(BF16)
    g3 = (r1 - g2.astype(F32)).astype(BF16)
    mats = mats_ref[...]
    stacked = (_dot(mats, g3) + _dot(mats, g2)) + _dot(mats, g1)
    return qs, kks, vs, stacked


def _rec_scores(qs, kks, vs, stacked, reverse):
    c = REC_CHUNK
    n_chunks = len(qs)
    row = lax.broadcasted_iota(I32, (c, HEAD_W), 0)
    ti = lax.broadcasted_iota(I32, (c, c), 0)
    si = lax.broadcasted_iota(I32, (c, c), 1)
    is_query, same = [], []
    for lv in REC_LEVELS:
        in_upper = (row & (2 * lv - 1)) >= lv
        is_query.append(jnp.logical_not(in_upper) if reverse else in_upper)
        shift = (2 * lv).bit_length() - 1
        same.append((ti >> shift) == (si >> shift))
    diag = ti == si

    o_intra, incs, qes, decays = [], [], [], []
    for n in range(n_chunks):
        lanes = slice(n * HEAD_W, (n + 1) * HEAD_W)
        b = stacked[0:c, lanes]
        q, kk = qs[n], kks[n]
        a = jnp.where(diag, _dot_nt(q.astype(BF16), kk.astype(BF16)), 0.0)
        for m, lv in enumerate(REC_LEVELS):
            if lv in REC_MAT_LEVELS:
                mi = REC_MAT_LEVELS.index(lv)
                r = stacked[(mi + 1) * c:(mi + 2) * c, lanes]
            else:
                r = jnp.concatenate(
                    [jnp.broadcast_to(b[bd:bd + 1], (2 * lv, HEAD_W))
                     for bd in (_rec_boundary_row(lv, grp, reverse) for grp in range(c // (2 * lv)))],
                    axis=0)
            e = jnp.exp2(-jnp.abs(b - r))
            qt = jnp.where(is_query[m], q * e, 0.0).astype(BF16)
            kt = jnp.where(is_query[m], 0.0, kk * e).astype(BF16)
            a = a + jnp.where(same[m], _dot_nt(qt, kt), 0.0)
        b_end = b[0:1] if reverse else b[c - 1:c]
        o_intra.append(_dot(a.astype(BF16), vs[n]))
        incs.append(_dot_tn(vs[n], (kk * jnp.exp2(b_end - b)).astype(BF16)))
        qes.append((q * jnp.exp2(b)).astype(BF16))
        decays.append(jnp.exp2(b_end))
    return o_intra, incs, qes, decays


def _rec_scan(o_intra, incs, qes, decays, o_ref, st_ref, reverse):
    n_chunks = len(o_intra)
    st = st_ref[...]
    order = range(n_chunks - 1, -1, -1) if reverse else range(n_chunks)
    for n in order:
        o_ref[n * REC_CHUNK:(n + 1) * REC_CHUNK, :] = o_intra[n] + _dot_nt(qes[n], st.astype(BF16))
        st = st * decays[n] + incs[n]
    st_ref[...] = st


def _rec_kernel(qf_ref, zf_ref, vf_ref, qb_ref, zb_ref, vb_ref, lbpf_ref, lbpb_ref, matsf_ref,
                matsb_ref, of_ref, ob_ref, stf_ref, stb_ref, *, n_chunks):
    @pl.when(pl.program_id(2) == 0)
    def _():
        stf_ref[...] = jnp.zeros(stf_ref.shape, F32)
        stb_ref[...] = jnp.zeros(stb_ref.shape, F32)

    gates_f = _rec_gates(qf_ref, zf_ref, vf_ref, lbpf_ref, matsf_ref, n_chunks)
    gates_b = _rec_gates(qb_ref, zb_ref, vb_ref, lbpb_ref, matsb_ref, n_chunks)
    parts_f = _rec_scores(*gates_f, False)
    parts_b = _rec_scores(*gates_b, True)
    _rec_scan(*parts_f, of_ref, stf_ref, False)
    _rec_scan(*parts_b, ob_ref, stb_ref, True)


def _recurrence(rec, rec_lower_bound, batch, seq, tb):
    nb = seq // tb
    lbp = rec_lower_bound.astype(F32)

    def fwd(col):
        return lambda b, h, i: (b * nb + i, col + h)

    def bwd(col):
        return lambda b, h, i: (b * nb + nb - 1 - i, col + h)

    blk = (tb, HEAD_W)
    mat_shape = ((len(REC_MAT_LEVELS) + 1) * REC_CHUNK, REC_CHUNK)
    out = jax.ShapeDtypeStruct((batch * seq, REC_W), F32)
    return pl.pallas_call(
        functools.partial(_rec_kernel, n_chunks=tb // REC_CHUNK),
        grid=(batch, HEADS, nb),
        in_specs=[
            pl.BlockSpec(blk, fwd(0)), pl.BlockSpec(blk, fwd(HEADS)), pl.BlockSpec(blk, fwd(3 * HEADS)),
            pl.BlockSpec(blk, bwd(0)), pl.BlockSpec(blk, bwd(2 * HEADS)), pl.BlockSpec(blk, bwd(3 * HEADS)),
            pl.BlockSpec((2, HEAD_W), lambda b, h, i: (0, h)),
            pl.BlockSpec((2, HEAD_W), lambda b, h, i: (0, h)),
            pl.BlockSpec(mat_shape, lambda b, h, i: (0, 0)),
            pl.BlockSpec(mat_shape, lambda b, h, i: (0, 0)),
        ],
        out_specs=[pl.BlockSpec(blk, fwd(0)), pl.BlockSpec(blk, bwd(0))],
        out_shape=[out, out],
        scratch_shapes=[pltpu.VMEM((HEAD_W, HEAD_W), F32), pltpu.VMEM((HEAD_W, HEAD_W), F32)],
        compiler_params=_cparams(("parallel", "parallel", "arbitrary")), name="rec_bidir",
    )(rec, rec, rec, rec, rec, rec, lbp[0], lbp[1], _rec_mats(False), _rec_mats(True))


def _split_bf16(x):
    hi = x.astype(BF16)
    lo = (x - hi.astype(F32)).astype(BF16)
    return hi, lo


def _outproj_kernel(x_ref, ao_ref, of_ref, ob_ref, rg_ref, rng_ref, woa_ref, wor_ref,
                    fg_ref, wrh_ref, wrl_ref, x1_ref, x1acc_ref, aff_ref):
    parts = []
    for hd in range(HEADS):
        sl = slice(hd * HEAD_W, (hd + 1) * HEAD_W)
        o = of_ref[:, sl] + ob_ref[:, sl]
        parts.append((_rms(o, rng_ref[:, sl]) * rg_ref[:, sl]).astype(BF16))
    ro = jnp.concatenate(parts, axis=-1)
    x1 = x_ref[...] + _dot(ao_ref[...], woa_ref[...]) + _dot(ro, wor_ref[...])
    x1_ref[...] = x1
    x1acc_ref[...] = x1
    h2 = _rms(x1, fg_ref[...])
    hi, lo = _split_bf16(h2)
    logits = _dot(hi, wrh_ref[...]) + _dot(lo, wrh_ref[...]) + _dot(hi, wrl_ref[...])
    lane = lax.broadcasted_iota(I32, logits.shape, 1)
    logits = jnp.where(lane < N_EXPERTS, logits, -jnp.inf)
    mx = jnp.max(logits, axis=-1, keepdims=True)
    ex = jnp.exp(logits - mx)
    aff = ex / jnp.sum(ex, axis=-1, keepdims=True)
    aff_ref[...] = aff.T[:N_EXPERTS]


def _outproj(x2d, ao, o_f, o_b, rec, rec_norm_g, w_out, norm_ffn_g, w_router, tm):
    n_tok = x2d.shape[0]
    w_bf = w_out.astype(BF16)
    wr = jnp.pad(w_router.astype(F32), ((0, 0), (0, LANES - N_EXPERTS)))
    wr_hi = wr.astype(BF16)
    wr_lo = (wr - wr_hi.astype(F32)).astype(BF16)
    row = lambda i: (i, 0)
    fixed = lambda i: (0, 0)
    return pl.pallas_call(
        _outproj_kernel,
        grid=(n_tok // tm,),
        in_specs=[
            pl.BlockSpec((tm, D_MODEL), row),
            pl.BlockSpec((tm, ATTN_W), row),
            pl.BlockSpec((tm, REC_W), row),
            pl.BlockSpec((tm, REC_W), row),
            pl.BlockSpec((tm, REC_W), lambda i: (i, 4)),
            pl.BlockSpec((1, REC_W), fixed),
            pl.BlockSpec((ATTN_W, D_MODEL), fixed),
            pl.BlockSpec((REC_W, D_MODEL), fixed),
            pl.BlockSpec((1, D_MODEL), fixed),
            pl.BlockSpec((D_MODEL, LANES), fixed),
            pl.BlockSpec((D_MODEL, LANES), fixed),
        ],
        out_specs=[pl.BlockSpec((tm, D_MODEL), row), pl.BlockSpec((tm, D_MODEL), row),
                   pl.BlockSpec((N_EXPERTS, tm), lambda i: (0, i))],
        out_shape=[jax.ShapeDtypeStruct((n_tok, D_MODEL), F32),
                   jax.ShapeDtypeStruct((n_tok, D_MODEL), F32),
                   jax.ShapeDtypeStruct((N_EXPERTS, n_tok), F32)],
        compiler_params=_cparams(("parallel",)), name="outproj_router",
    )(x2d, ao, o_f, o_b, rec, rec_norm_g.astype(F32).reshape(1, REC_W),
      w_bf[:ATTN_W], w_bf[ATTN_W:], norm_ffn_g.astype(F32).reshape(1, D_MODEL), wr_hi, wr_lo)


TOPK_BLK = 256


def _topk_kernel(aff_ref, tri_ref, idx_ref, gate_ref, *, n_tok, cap):
    aff = aff_ref[...]

    def search(step, cand):
        trial = cand | (jnp.int32(1) << (30 - step))
        n_ge = jnp.sum((aff >= pltpu.bitcast(trial, F32)).astype(I32), axis=1, keepdims=True)
        return jnp.where(n_ge >= cap, trial, cand)

    thr = pltpu.bitcast(lax.fori_loop(0, 31, search, jnp.zeros((N_EXPERTS, 1), I32)), F32)
    gt = aff > thr
    tie = aff == thr
    need = (cap - jnp.sum(gt.astype(I32), axis=1, keepdims=True)).astype(F32)

    def running_count(flags):
        f = flags.astype(F32).astype(BF16)
        carry = jnp.zeros((N_EXPERTS, 1), F32)
        out = []
        for j in range(n_tok // TOPK_BLK):
            pre = _dot(f[:, j * TOPK_BLK:(j + 1) * TOPK_BLK], tri_ref[...]) + carry
            out.append(pre)
            carry = pre[:, TOPK_BLK - 1:TOPK_BLK]
        return jnp.concatenate(out, axis=1)

    sel = jnp.logical_or(gt, jnp.logical_and(tie, running_count(tie) <= need))
    slot = running_count(sel).astype(I32) - 1

    lane = lax.broadcasted_iota(I32, (N_EXPERTS, n_tok), 1)
    packed = jnp.where(sel, (slot << 16) | lane, -1)
    val = aff
    for k in range((n_tok - 1).bit_length()):
        step = 1 << k
        dist = lane - (packed >> 16)
        moving = jnp.logical_and(packed >= 0, ((dist >> k) & 1) == 1)
        arriving = pltpu.roll(jnp.where(moving, packed, -1), n_tok - step, axis=1)
        arriving_val = pltpu.roll(val, n_tok - step, axis=1)
        lands = arriving >= 0
        packed = jnp.where(lands, arriving, jnp.where(moving, -1, packed))
        val = jnp.where(lands, arriving_val, val)
    idx_ref[...] = packed[:, :cap] & 0xFFFF
    gate_ref[...] = val[:, :cap]


def _topk(aff_t, cap):
    n_tok = aff_t.shape[1]
    assert n_tok <= 32768 and n_tok % TOPK_BLK == 0, "token index and slot share one int32"
    r = np.arange(TOPK_BLK)
    tri = jnp.asarray((r[:, None] <= r[None, :]).astype(np.float32), dtype=BF16)
    return pl.pallas_call(
        functools.partial(_topk_kernel, n_tok=n_tok, cap=cap),
        out_shape=[jax.ShapeDtypeStruct((N_EXPERTS, cap), I32),
                   jax.ShapeDtypeStruct((N_EXPERTS, cap), F32)],
        compiler_params=pltpu.CompilerParams(vmem_limit_bytes=VMEM_LIMIT), name="expert_topk",
    )(aff_t, tri)


ROW_DMA_UNROLL = 8


def _row_copy(hbm, vmem, t, p, sem, gather):
    if gather:
        return pltpu.make_async_copy(hbm.at[pl.ds(t, 1), :], vmem.at[pl.ds(p, 1), :], sem)
    return pltpu.make_async_copy(vmem.at[pl.ds(p, 1), :], hbm.at[pl.ds(t, 1), :], sem)


def _rows_start_loop(hbm, vmem, idx_ref, base, cap, sem, gather):
    def issue(blk, carry):
        for u in range(ROW_DMA_UNROLL):
            p = blk * ROW_DMA_UNROLL + u
            _row_copy(hbm, vmem, idx_ref[base + p], p, sem, gather).start()
        return carry
    lax.fori_loop(0, cap // ROW_DMA_UNROLL, issue, 0)


def _rows_start_inline(hbm, vmem, idx_ref, base, p0, count, sem, gather):
    for u in range(count):
        _row_copy(hbm, vmem, idx_ref[base + p0 + u], p0 + u, sem, gather).start()


def _rows_wait(hbm, vmem, cap, sem, gather):
    if gather:
        pltpu.make_async_copy(hbm.at[pl.ds(0, cap), :], vmem, sem).wait()
    else:
        pltpu.make_async_copy(vmem, hbm.at[pl.ds(0, cap), :], sem).wait()


def _moe_kernel(idx_ref, gate_ref, fg_ref, x_hbm, acc_in_hbm, wg_ref, wu_ref, wd_ref, out_hbm,
                xg, rows, xe, ye, sem_x, sem_r, sem_s, *, cap, nf):
    del acc_in_hbm
    e = pl.program_id(0)
    f = pl.program_id(1)
    last_e = pl.num_programs(0) - 1
    chunk = cap // nf

    @pl.when(f == 0)
    def _():
        @pl.when(e == 0)
        def _():
            _rows_start_loop(x_hbm, xg, idx_ref, 0, cap, sem_x, True)
            _rows_wait(x_hbm, xg, cap, sem_x, True)

        @pl.when(e > 0)
        def _():
            _rows_wait(out_hbm, rows, cap, sem_s, False)

        xe[...] = _rms(xg[...], fg_ref[...]).astype(BF16)
        ye[...] = jnp.zeros(ye.shape, F32)

    _rows_start_inline(x_hbm, xg, idx_ref, jnp.minimum(e + 1, last_e) * cap, f * chunk, chunk,
                       sem_x, True)
    _rows_start_inline(out_hbm, rows, idx_ref, e * cap, f * chunk, chunk, sem_r, True)

    xb = xe[...]
    hg = _dot(xb, wg_ref[0].astype(BF16))
    hu = _dot(xb, wu_ref[0].astype(BF16))
    hid = hg * (1.0 / (1.0 + jnp.exp(-hg))) * hu
    ye[...] += _dot(hid.astype(BF16), wd_ref[0].astype(BF16))

    @pl.when(f == nf - 1)
    def _():
        _rows_wait(x_hbm, xg, cap, sem_x, True)
        _rows_wait(out_hbm, rows, cap, sem_r, True)
        rows[...] = rows[...] + ye[...] * gate_ref[0]
        _rows_start_inline(out_hbm, rows, idx_ref, e * cap, 0, cap, sem_s, False)

        @pl.when(e == last_e)
        def _():
            _rows_wait(out_hbm, rows, cap, sem_s, False)


def _moe(x1, x1_acc, idx, gates, norm_ffn_g, w_gate, w_up, w_down, tf):
    n_tok = x1.shape[0]
    cap = idx.shape[1]
    nf = D_EXPERT // tf
    grid_spec = pltpu.PrefetchScalarGridSpec(
        num_scalar_prefetch=1,
        grid=(N_EXPERTS, nf),
        in_specs=[
            pl.BlockSpec((1, cap, 1), lambda e, f, s: (e, 0, 0)),
            pl.BlockSpec((1, D_MODEL), lambda e, f, s: (0, 0)),
            pl.BlockSpec(memory_space=pl.ANY),
            pl.BlockSpec(memory_space=pl.ANY),
            pl.BlockSpec((1, D_MODEL, tf), lambda e, f, s: (e, 0, f)),
            pl.BlockSpec((1, D_MODEL, tf), lambda e, f, s: (e, 0, f)),
            pl.BlockSpec((1, tf, D_MODEL), lambda e, f, s: (e, f, 0)),
        ],
        out_specs=pl.BlockSpec(memory_space=pl.ANY),
        scratch_shapes=[pltpu.VMEM((cap, D_MODEL), F32), pltpu.VMEM((cap, D_MODEL), F32),
                        pltpu.VMEM((cap, D_MODEL), BF16), pltpu.VMEM((cap, D_MODEL), F32),
                        pltpu.SemaphoreType.DMA(()), pltpu.SemaphoreType.DMA(()),
                        pltpu.SemaphoreType.DMA(())],
    )
    return pl.pallas_call(
        functools.partial(_moe_kernel, cap=cap, nf=nf),
        grid_spec=grid_spec,
        out_shape=jax.ShapeDtypeStruct((n_tok, D_MODEL), F32),
        input_output_aliases={4: 0},
        compiler_params=_cparams(("arbitrary", "arbitrary")), name="expert_ffn",
    )(idx.reshape(N_EXPERTS * cap), gates.reshape(N_EXPERTS, cap, 1),
      norm_ffn_g.astype(F32).reshape(1, D_MODEL),
      x1, x1_acc, w_gate, w_up, w_down)


def _ple_kernel(x_ref, p_ref, g_ref, wg_ref, wp_ref, o_ref):
    x = x_ref[...]
    zg = _dot(_rms(x, g_ref[...]).astype(BF16), wg_ref[...])
    gate = 1.0 / (1.0 + jnp.exp(-zg))
    o_ref[...] = x + gate * _dot(p_ref[...].astype(BF16), wp_ref[...])


def _ple(x2, p2d, norm_ple_g, w_ple_gate, w_ple_proj, tm):
    n_tok = x2.shape[0]
    row = lambda i: (i, 0)
    fixed = lambda i: (0, 0)
    return pl.pallas_call(
        _ple_kernel,
        grid=(n_tok // tm,),
        in_specs=[pl.BlockSpec((tm, D_MODEL), row), pl.BlockSpec((tm, PLE_DIM), row),
                  pl.BlockSpec((1, D_MODEL), fixed), pl.BlockSpec((D_MODEL, D_MODEL), fixed),
                  pl.BlockSpec((PLE_DIM, D_MODEL), fixed)],
        out_specs=pl.BlockSpec((tm, D_MODEL), row),
        out_shape=jax.ShapeDtypeStruct((n_tok, D_MODEL), F32),
        compiler_params=_cparams(("parallel",)), name="ple_gate",
    )(x2, p2d, norm_ple_g.astype(F32).reshape(1, D_MODEL), w_ple_gate.astype(BF16),
      w_ple_proj.astype(BF16))


def _tiles(batch, seq):
    t_attn = min(512, seq)
    n_blk = batch * seq // t_attn
    t_proj = t_attn * (2 if n_blk % 2 == 0 else 1)
    t_rec = min(512, seq)
    t_row = min(256, seq)
    return t_attn, t_proj, t_rec, t_row


def _layer(x, p, norm_mix_g, w_in, q_norm_g, k_norm_g, diff_lambda, diff_norm_g, rec_lower_bound,
           rec_norm_g, w_out, norm_ffn_g, w_router, w_expert_gate, w_expert_up, w_expert_down,
           norm_ple_g, w_ple_gate, w_ple_proj):
    batch, seq, _ = x.shape
    n_tok = batch * seq
    t_attn, t_proj, t_rec, t_row = _tiles(batch, seq)
    x2d = x.reshape(n_tok, D_MODEL)
    qT, k, vT, rec = _projections(x2d, norm_mix_g[0], w_in[0], q_norm_g[0], k_norm_g[0], t_proj,
                                  t_attn)
    ao = _attention(qT, k, vT, diff_lambda[0], diff_norm_g[0], batch, seq, t_attn)
    o_f, o_b = _recurrence(rec, rec_lower_bound, batch, seq, t_rec)
    x1, x1_acc, aff = _outproj(x2d, ao, o_f, o_b, rec, rec_norm_g[0], w_out[0], norm_ffn_g[0],
                               w_router[0], t_row)
    cap = max(1, (CAPACITY_FACTOR * n_tok) // N_EXPERTS)
    idx, gates = _topk(aff, cap)
    x2 = _moe(x1, x1_acc, idx, gates, norm_ffn_g[0], w_expert_gate[0], w_expert_up[0],
              w_expert_down[0], tf=256)
    y = _ple(x2, p[0].reshape(n_tok, PLE_DIM), norm_ple_g[0], w_ple_gate[0], w_ple_proj[0], t_row)
    return y.reshape(x.shape)


def kernel(x_prompt, x_sample, p_prompt, p_sample, norm_mix_g, w_in, q_norm_g, k_norm_g, diff_lambda, diff_norm_g, rec_lower_bound, rec_norm_g, w_out, norm_ffn_g, w_router, w_expert_gate, w_expert_up, w_expert_down, norm_ple_g, w_ple_gate, w_ple_proj):
    weights = (norm_mix_g, w_in, q_norm_g, k_norm_g, diff_lambda, diff_norm_g, rec_lower_bound,
               rec_norm_g, w_out, norm_ffn_g, w_router, w_expert_gate, w_expert_up, w_expert_down,
               norm_ple_g, w_ple_gate, w_ple_proj)
    return (_layer(x_prompt, p_prompt, *weights), _layer(x_sample, p_sample, *weights))
```

```python
import functools
import math

import numpy as np
import jax
import jax.numpy as jnp
from jax import lax
from jax.experimental import pallas as pl
from jax.experimental.pallas import tpu as pltpu

F32 = jnp.float32
BF16 = jnp.bfloat16
I32 = jnp.int32

D_MODEL = 2048
PLE_DIM = 256
HEADS = 8
QK_DIM = 64
V_DIM = 128
HEAD_W = 128
ATTN_W = HEADS * V_DIM
REC_W = HEADS * V_DIM
N_EXPERTS = 16
CAPACITY_FACTOR = 2
D_EXPERT = 2048
NORM_EPS = 1e-6
LAM_INIT = 0.8 - 0.6 * math.exp(-0.3 * 0)

LANES = 128
LOG2E = 1.4426950408889634
VMEM_LIMIT = 56 * 1024 * 1024

REC_CHUNK = 64
REC_LEVELS = (1, 2, 4, 8, 16, 32)


def _cparams(sem, vmem=VMEM_LIMIT):
    return pltpu.CompilerParams(dimension_semantics=sem, vmem_limit_bytes=vmem)


def _dot(a, b):
    return jnp.dot(a, b, preferred_element_type=F32)


def _dot_nt(a, b):
    return lax.dot_general(a, b, (((1,), (1,)), ((), ())), preferred_element_type=F32)


def _dot_tn(a, b):
    return lax.dot_general(a, b, (((0,), (0,)), ((), ())), preferred_element_type=F32)


def _rms(x, g):
    ms = jnp.mean(x * x, axis=-1, keepdims=True)
    return x * lax.rsqrt(ms + NORM_EPS) * g


PROJ_TN = 512
QK_GROUP_TILE = 256
Q_BLKS = ATTN_W // PROJ_TN
REC_SEGS = 5


def _proj_kernel(x_ref, g_ref, w_ref, qg_ref, kg_ref, bd_ref, qT_ref, k_ref, vT_ref, rec_ref, h_ref,
                 *, tq):
    j = pl.program_id(1)

    @pl.when(j == 0)
    def _():
        h_ref[...] = _rms(x_ref[...], g_ref[...]).astype(BF16)

    tm = h_ref.shape[0]

    def project():
        return _dot(h_ref[...], w_ref[...])

    def qk_norm(gain_row):
        acc = project()
        parts = []
        for c0 in range(0, PROJ_TN, QK_GROUP_TILE):
            sub = acc[:, c0:c0 + QK_GROUP_TILE]
            ss = _dot((sub * sub).astype(BF16), bd_ref[...])
            parts.append(sub * lax.rsqrt(ss * (1.0 / QK_DIM) + NORM_EPS))
        return jnp.concatenate(parts, axis=1) * gain_row

    def store_transposed(o_ref, val):
        for r in range(tm // tq):
            o_ref[r] = val[r * tq:(r + 1) * tq, :].T.astype(o_ref.dtype)

    @pl.when(j < Q_BLKS)
    def _():
        store_transposed(qT_ref, qk_norm(qg_ref[...]))

    @pl.when(jnp.logical_and(j >= Q_BLKS, j < 2 * Q_BLKS))
    def _():
        k_ref[...] = qk_norm(kg_ref[...]).astype(k_ref.dtype)

    @pl.when(jnp.logical_and(j >= 2 * Q_BLKS, j < 3 * Q_BLKS))
    def _():
        store_transposed(vT_ref, project())

    @pl.when(j >= 3 * Q_BLKS)
    def _():
        acc = project()
        seg = (j - 3 * Q_BLKS) // Q_BLKS
        is_silu = jnp.logical_or(seg == 0, seg == REC_SEGS - 1)
        rec_ref[...] = jnp.where(is_silu, acc * (1.0 / (1.0 + jnp.exp(-acc))), acc)


def _projections(x2d, norm_g, w_in, q_norm_g, k_norm_g, tm, tq):
    n_tok = x2d.shape[0]
    qg_row = jnp.tile(q_norm_g.astype(F32) * (QK_DIM ** -0.5 * LOG2E), 2 * HEADS).reshape(1, ATTN_W)
    kg_row = jnp.tile(k_norm_g.astype(F32), 2 * HEADS).reshape(1, ATTN_W)
    grp = np.arange(QK_GROUP_TILE) // QK_DIM
    bd = jnp.asarray((grp[:, None] == grp[None, :]).astype(np.float32), dtype=BF16)
    n_col_blocks = (3 * ATTN_W + REC_SEGS * REC_W) // PROJ_TN

    def seg_block(first):
        def index(j, n_blocks):
            return jnp.clip(j - first, 0, n_blocks - 1)
        return index

    q_blk, k_blk, v_blk, r_blk = (seg_block(0), seg_block(Q_BLKS), seg_block(2 * Q_BLKS),
                                  seg_block(3 * Q_BLKS))
    t_shape = (tm // tq, PROJ_TN, tq)
    return pl.pallas_call(
        functools.partial(_proj_kernel, tq=tq),
        grid=(n_tok // tm, n_col_blocks),
        in_specs=[
            pl.BlockSpec((tm, D_MODEL), lambda i, j: (i, 0)),
            pl.BlockSpec((1, D_MODEL), lambda i, j: (0, 0)),
            pl.BlockSpec((D_MODEL, PROJ_TN), lambda i, j: (0, j)),
            pl.BlockSpec((1, PROJ_TN), lambda i, j: (0, q_blk(j, Q_BLKS))),
            pl.BlockSpec((1, PROJ_TN), lambda i, j: (0, k_blk(j, Q_BLKS))),
            pl.BlockSpec((QK_GROUP_TILE, QK_GROUP_TILE), lambda i, j: (0, 0)),
        ],
        out_specs=[
            pl.BlockSpec(t_shape, lambda i, j: (i, q_blk(j, Q_BLKS), 0)),
            pl.BlockSpec((tm, PROJ_TN), lambda i, j: (i, k_blk(j, Q_BLKS))),
            pl.BlockSpec(t_shape, lambda i, j: (i, v_blk(j, Q_BLKS), 0)),
            pl.BlockSpec((tm, PROJ_TN), lambda i, j: (i, r_blk(j, REC_SEGS * Q_BLKS))),
        ],
        out_shape=[
            jax.ShapeDtypeStruct((n_tok // tq, ATTN_W, tq), BF16),
            jax.ShapeDtypeStruct((n_tok, ATTN_W), BF16),
            jax.ShapeDtypeStruct((n_tok // tq, ATTN_W, tq), BF16),
            jax.ShapeDtypeStruct((n_tok, REC_SEGS * REC_W), F32),
        ],
        scratch_shapes=[pltpu.VMEM((tm, D_MODEL), BF16)],
        compiler_params=_cparams(("parallel", "arbitrary")), name="in_proj",
    )(x2d, norm_g.reshape(1, D_MODEL), w_in.astype(BF16), qg_row, kg_row, bd)


L_ROWS = 16
SKIP_LOG2_MARGIN = 160.0
SKIP_NORM_SLACK = 1.02


def _attn_kernel(slopes_ref, qT_ref, k_ref, vT_ref, lam_ref, g_ref, o_ref,
                 m_ref, acc_ref, bias_ref, ta_ref, tb_ref, mta_ref, mtb_ref, kn_ref, *, nk, tk, tq):
    h = pl.program_id(1)
    i = pl.program_id(2)
    sl = slopes_ref[h]
    lane_h = lax.broadcasted_iota(I32, (1, HEAD_W), 1)
    lane_v = lax.broadcasted_iota(I32, (1, LANES), 1)

    @pl.when(i == 0)
    def _():
        rel = (lax.broadcasted_iota(I32, (tk, tq), 1)
               - lax.broadcasted_iota(I32, (tk, tq), 0)).astype(F32)
        sr = sl * rel
        sa = sl * jnp.abs(rel)
        bias_ref[0] = jnp.concatenate([sr, sr], axis=1)
        bias_ref[1] = jnp.concatenate([sa, sa], axis=1)
        bias_ref[2] = jnp.concatenate([-sr, -sr], axis=1)

        def block_norm(j, carry):
            kf = k_ref[pl.ds(pl.multiple_of(j * tk, tk), tk), :].astype(F32)
            sq = kf * kf
            out = []
            for c in range(2):
                in_map = (lane_h >= c * QK_DIM) & (lane_h < (c + 1) * QK_DIM)
                n2 = jnp.sum(jnp.where(in_map, sq, 0.0), axis=1, keepdims=True)
                nmax = jnp.sqrt(jnp.max(n2, axis=0, keepdims=True))
                out.append(jnp.where(lane_v == j, nmax, carry[c]))
            return tuple(out)

        kn0, kn1 = lax.fori_loop(0, nk, block_norm,
                                 (jnp.zeros((1, LANES), F32), jnp.zeros((1, LANES), F32)))
        kn_ref[0:1, :] = kn0
        kn_ref[1:2, :] = kn1

    qT = qT_ref[0]
    zero_half = jnp.zeros((QK_DIM, tq), qT.dtype)
    q_cat = jnp.concatenate([jnp.concatenate([qT[:QK_DIM], zero_half], axis=0),
                             jnp.concatenate([zero_half, qT[QK_DIM:]], axis=0)], axis=1)
    ones = jnp.ones((L_ROWS, tk), BF16)
    m_ref[...] = jnp.full(m_ref.shape, -jnp.inf, F32)
    acc_ref[...] = jnp.zeros(acc_ref.shape, F32)

    def shift_of(j):
        gap = jnp.abs(jnp.full((1, 2 * tq), (i - j) * tq, I32)).astype(F32)
        return -sl * gap

    def scores(j, t_ref, mt_ref):
        kblk = k_ref[pl.ds(pl.multiple_of(j * tk, tk), tk), :]
        side = jnp.where(j < i, 0, jnp.where(j == i, 1, 2))
        t = _dot(kblk, q_cat) - bias_ref[side]
        t_ref[...] = t
        mt_ref[...] = jnp.max(t, axis=0, keepdims=True) + shift_of(j)

    def accumulate(j, t_ref, mt_ref):
        m_old = m_ref[...]
        m_new = jnp.maximum(m_old, mt_ref[...])
        p = jnp.exp2(t_ref[...] - (m_new - shift_of(j)))
        v_aug = jnp.concatenate([vT_ref[j], ones], axis=0)
        acc_ref[...] = jnp.exp2(m_old - m_new) * acc_ref[...] + _dot(v_aug, p.astype(BF16))
        m_ref[...] = m_new

    scores(i, ta_ref, mta_ref)

    qf = qT.astype(F32)
    qsq = qf * qf
    thr = None
    for c in range(2):
        rows_c = slice(c * QK_DIM, (c + 1) * QK_DIM)
        bq = jnp.sqrt(jnp.max(jnp.sum(qsq[rows_c], axis=0, keepdims=True), axis=1, keepdims=True))
        kn = kn_ref[c:c + 1, :]
        k_all = jnp.max(kn, axis=1, keepdims=True)
        k_diag = jnp.max(jnp.where(lane_v == i, kn, 0.0), axis=1, keepdims=True)
        thr_c = SKIP_NORM_SLACK * bq * (k_all + k_diag) + SKIP_LOG2_MARGIN
        thr = thr_c if thr is None else jnp.maximum(thr, thr_c)
    dist = lane_v.astype(F32)
    visit = (lane_v >= 1) & (sl * ((dist - 1.0) * tq + 1.0) <= thr)
    radius = jnp.sum(visit.astype(I32))
    lo = jnp.maximum(i - radius, 0)
    hi = jnp.minimum(i + radius, nk - 1)
    n_vis = hi - lo + 1

    def visited(n):
        j = lo + n - 1
        return jnp.where(n == 0, i, jnp.where(j < i, j, j + 1))

    def pair(jj, carry):
        n = 2 * jj
        scores(visited(n + 1), tb_ref, mtb_ref)
        accumulate(visited(n), ta_ref, mta_ref)
        scores(visited(n + 2), ta_ref, mta_ref)
        accumulate(visited(n + 1), tb_ref, mtb_ref)
        return carry

    lax.fori_loop(0, jnp.right_shift(n_vis - 1, 1), pair, 0)

    @pl.when(jnp.bitwise_and(n_vis, 1) == 1)
    def _():
        accumulate(visited(n_vis - 1), ta_ref, mta_ref)

    @pl.when(jnp.bitwise_and(n_vis, 1) == 0)
    def _():
        scores(visited(n_vis - 1), tb_ref, mtb_ref)
        accumulate(visited(n_vis - 2), ta_ref, mta_ref)
        accumulate(visited(n_vis - 1), tb_ref, mtb_ref)

    lp = lam_ref[...]
    lam = (jnp.exp(jnp.sum(lp[0:1] * lp[1:2], axis=-1, keepdims=True))
           - jnp.exp(jnp.sum(lp[2:3] * lp[3:4], axis=-1, keepdims=True)) + LAM_INIT)
    acc = acc_ref[...]
    a0 = acc[:, :tq]
    a1 = acc[:, tq:]
    o = (a0[:V_DIM] / a0[V_DIM:V_DIM + 1]
         - lam * (a1[:V_DIM] / a1[V_DIM:V_DIM + 1]))
    ms = jnp.mean(o * o, axis=0, keepdims=True)
    on = o * lax.rsqrt(ms + NORM_EPS) * g_ref[...] * (1.0 - LAM_INIT)
    o_ref[...] = on.T.astype(o_ref.dtype)


def _attention(qT, k, vT, diff_lambda, diff_norm_g, batch, seq, tq):
    nq = seq // tq
    assert nq <= LANES, "per-block key norms are kept one per lane"
    slopes = LOG2E * jnp.exp2(-8.0 * jnp.arange(1, HEADS + 1, dtype=F32) / HEADS)
    g_col = diff_norm_g.astype(F32).reshape(ATTN_W, 1)
    grid_spec = pltpu.PrefetchScalarGridSpec(
        num_scalar_prefetch=1,
        grid=(batch, HEADS, nq),
        in_specs=[
            pl.BlockSpec((1, HEAD_W, tq), lambda b, h, i, s: (b * nq + i, h, 0)),
            pl.BlockSpec((seq, HEAD_W), lambda b, h, i, s: (b, h)),
            pl.BlockSpec((nq, HEAD_W, tq), lambda b, h, i, s: (b, h, 0)),
            pl.BlockSpec((4, QK_DIM), lambda b, h, i, s: (0, 0)),
            pl.BlockSpec((HEAD_W, 1), lambda b, h, i, s: (h, 0)),
        ],
        out_specs=pl.BlockSpec((tq, HEAD_W), lambda b, h, i, s: (b * nq + i, h)),
        scratch_shapes=[pltpu.VMEM((1, 2 * tq), F32),
                        pltpu.VMEM((V_DIM + L_ROWS, 2 * tq), F32),
                        pltpu.VMEM((3, tq, 2 * tq), F32),
                        pltpu.VMEM((tq, 2 * tq), F32), pltpu.VMEM((tq, 2 * tq), F32),
                        pltpu.VMEM((1, 2 * tq), F32), pltpu.VMEM((1, 2 * tq), F32),
                        pltpu.VMEM((2, LANES), F32)],
    )
    return pl.pallas_call(
        functools.partial(_attn_kernel, nk=nq, tk=tq, tq=tq),
        grid_spec=grid_spec,
        out_shape=jax.ShapeDtypeStruct((batch * seq, ATTN_W), BF16),
        compiler_params=_cparams(("arbitrary", "arbitrary", "arbitrary")), name="diff_attn",
    )(slopes, qT, k, vT, diff_lambda.astype(F32), g_col)


SUBLANES = 8
REC_MAT_LEVELS = tuple(lv for lv in REC_LEVELS if 4 * lv < SUBLANES)


def _rec_boundary_row(lv, group, reverse):
    return group * 2 * lv + (lv if reverse else lv - 1)


def _rec_mats(reverse):
    c = REC_CHUNK
    t = np.arange(c)[:, None]
    u = np.arange(c)[None, :]
    mats = [(u >= t) if reverse else (u <= t)]
    for lv in REC_MAT_LEVELS:
        bd = _rec_boundary_row(lv, t // (2 * lv), reverse)
        mats.append((u >= bd) if reverse else (u <= bd))
    return jnp.asarray(np.concatenate(mats, axis=0).astype(np.float32), dtype=BF16)


def _rec_gates(q_ref, z_ref, v_ref, lbp_ref, mats_ref, n_chunks):
    c = REC_CHUNK
    lbp = lbp_ref[...]
    mx = jnp.maximum(lbp[0:1], lbp[1:2])
    e0 = jnp.exp(lbp[0:1] - mx)
    e1 = jnp.exp(lbp[1:2] - mx)
    lb = e0 / (e0 + e1)

    rows = [slice(n * c, (n + 1) * c) for n in range(n_chunks)]
    qs = [q_ref[sl, :] for sl in rows]
    vs = [v_ref[sl, :].astype(BF16) for sl in rows]
    gs, kks = [], []
    for sl in rows:
        z = z_ref[sl, :]
        sig = 1.0 / (1.0 + jnp.exp(-z))
        gs.append(jnp.log2(lb + (1.0 - lb) * sig))
        kks.append((1.0 - lb) * (1.0 / (1.0 + jnp.exp(z))))

    g_cat = jnp.concatenate(gs, axis=1)
    g1 = g_cat.astype(BF16)
    r1 = g_cat - g1.astype(F32)
    g2 = r1.astype(BF16)
    g3 = (r1 - g2.astype(F32)).astype(BF16)
    mats = mats_ref[...]
    stacked = (_dot(mats, g3) + _dot(mats, g2)) + _dot(mats, g1)
    return qs, kks, vs, stacked


def _rec_scores(qs, kks, vs, stacked, reverse):
    c = REC_CHUNK
    n_chunks = len(qs)
    row = lax.broadcasted_iota(I32, (c, HEAD_W), 0)
    ti = lax.broadcasted_iota(I32, (c, c), 0)
    si = lax.broadcasted_iota(I32, (c, c), 1)
    is_query, same = [], []
    for lv in REC_LEVELS:
        in_upper = (row & (2 * lv - 1)) >= lv
        is_query.append(jnp.logical_not(in_upper) if reverse else in_upper)
        shift = (2 * lv).bit_length() - 1
        same.append((ti >> shift) == (si >> shift))
    diag = ti == si

    o_intra, incs, qes, decays = [], [], [], []
    for n in range(n_chunks):
        lanes = slice(n * HEAD_W, (n + 1) * HEAD_W)
        b = stacked[0:c, lanes]
        q, kk = qs[n], kks[n]
        q_bf, kk_bf = q.astype(BF16), kk.astype(BF16)
        a = jnp.where(diag, _dot_nt(q_bf, kk_bf), 0.0)
        for m, lv in enumerate(REC_LEVELS):
            if lv in REC_MAT_LEVELS:
                mi = REC_MAT_LEVELS.index(lv)
                r = stacked[(mi + 1) * c:(mi + 2) * c, lanes]
            else:
                r = jnp.concatenate(
                    [jnp.broadcast_to(b[bd:bd + 1], (2 * lv, HEAD_W))
                     for bd in (_rec_boundary_row(lv, grp, reverse) for grp in range(c // (2 * lv)))],
                    axis=0)
            e = jnp.exp2(-jnp.abs(b - r))
            qt = jnp.where(is_query[m], q * e, 0.0).astype(BF16)
            kt = jnp.where(is_query[m], 0.0, kk * e).astype(BF16)
            part = _dot_nt(qt, kt)
            a = a + (part if 2 * lv == c else jnp.where(same[m], part, 0.0))
        b_end = b[0:1] if reverse else b[c - 1:c]
        o_intra.append(_dot(a.astype(BF16), vs[n]))
        incs.append(_dot_tn(vs[n], (kk * jnp.exp2(b_end - b)).astype(BF16)))
        qes.append((q * jnp.exp2(b)).astype(BF16))
        decays.append(jnp.exp2(b_end))
    return o_intra, incs, qes, decays


def _rec_scan(o_intra, incs, qes, decays, o_ref, st_ref, reverse):
    n_chunks = len(o_intra)
    st = st_ref[...]
    order = range(n_chunks - 1, -1, -1) if reverse else range(n_chunks)
    for n in order:
        o_ref[n * REC_CHUNK:(n + 1) * REC_CHUNK, :] = o_intra[n] + _dot_nt(qes[n], st.astype(BF16))
        st = st * decays[n] + incs[n]
    st_ref[...] = st


def _rec_kernel(qf_ref, zf_ref, vf_ref, qb_ref, zb_ref, vb_ref, lbpf_ref, lbpb_ref, matsf_ref,
                matsb_ref, of_ref, ob_ref, stf_ref, stb_ref, *, n_chunks):
    @pl.when(pl.program_id(2) == 0)
    def _():
        stf_ref[...] = jnp.zeros(stf_ref.shape, F32)
        stb_ref[...] = jnp.zeros(stb_ref.shape, F32)

    gates_f = _rec_gates(qf_ref, zf_ref, vf_ref, lbpf_ref, matsf_ref, n_chunks)
    gates_b = _rec_gates(qb_ref, zb_ref, vb_ref, lbpb_ref, matsb_ref, n_chunks)
    parts_f = _rec_scores(*gates_f, False)
    parts_b = _rec_scores(*gates_b, True)
    _rec_scan(*parts_f, of_ref, stf_ref, False)
    _rec_scan(*parts_b, ob_ref, stb_ref, True)


def _recurrence(rec, rec_lower_bound, batch, seq, tb):
    nb = seq // tb
    lbp = rec_lower_bound.astype(F32)

    def fwd(col):
        return lambda b, h, i: (b * nb + i, col + h)

    def bwd(col):
        return lambda b, h, i: (b * nb + nb - 1 - i, col + h)

    blk = (tb, HEAD_W)
    mat_shape = ((len(REC_MAT_LEVELS) + 1) * REC_CHUNK, REC_CHUNK)
    out = jax.ShapeDtypeStruct((batch * seq, REC_W), F32)
    return pl.pallas_call(
        functools.partial(_rec_kernel, n_chunks=tb // REC_CHUNK),
        grid=(batch, HEADS, nb),
        in_specs=[
            pl.BlockSpec(blk, fwd(0)), pl.BlockSpec(blk, fwd(HEADS)), pl.BlockSpec(blk, fwd(3 * HEADS)),
            pl.BlockSpec(blk, bwd(0)), pl.BlockSpec(blk, bwd(2 * HEADS)), pl.BlockSpec(blk, bwd(3 * HEADS)),
            pl.BlockSpec((2, HEAD_W), lambda b, h, i: (0, h)),
            pl.BlockSpec((2, HEAD_W), lambda b, h, i: (0, h)),
            pl.BlockSpec(mat_shape, lambda b, h, i: (0, 0)),
            pl.BlockSpec(mat_shape, lambda b, h, i: (0, 0)),
        ],
        out_specs=[pl.BlockSpec(blk, fwd(0)), pl.BlockSpec(blk, bwd(0))],
        out_shape=[out, out],
        scratch_shapes=[pltpu.VMEM((HEAD_W, HEAD_W), F32), pltpu.VMEM((HEAD_W, HEAD_W), F32)],
        compiler_params=_cparams(("parallel", "parallel", "arbitrary")), name="rec_bidir",
    )(rec, rec, rec, rec, rec, rec, lbp[0], lbp[1], _rec_mats(False), _rec_mats(True))


def _split_bf16(x):
    hi = x.astype(BF16)
    lo = (x - hi.astype(F32)).astype(BF16)
    return hi, lo


def _outproj_kernel(x_ref, ao_ref, of_ref, ob_ref, rg_ref, rng_ref, woa_ref, wor_ref,
                    fg_ref, wr_ref, x1_ref, x1acc_ref, aff_ref):
    parts = []
    for hd in range(HEADS):
        sl = slice(hd * HEAD_W, (hd + 1) * HEAD_W)
        o = of_ref[:, sl] + ob_ref[:, sl]
        parts.append((_rms(o, rng_ref[:, sl]) * rg_ref[:, sl]).astype(BF16))
    ro = jnp.concatenate(parts, axis=-1)
    x1 = x_ref[...] + _dot(ao_ref[...], woa_ref[...]) + _dot(ro, wor_ref[...])
    x1_ref[...] = x1
    x1acc_ref[...] = x1
    h2 = _rms(x1, fg_ref[...])
    hi, lo = _split_bf16(h2)
    logits = _dot(hi, wr_ref[...]) + _dot(lo, wr_ref[...])
    lane = lax.broadcasted_iota(I32, logits.shape, 1)
    logits = jnp.where(lane < N_EXPERTS, logits, -jnp.inf)
    mx = jnp.max(logits, axis=-1, keepdims=True)
    ex = jnp.exp(logits - mx)
    aff = ex / jnp.sum(ex, axis=-1, keepdims=True)
    aff_ref[...] = aff.T[:N_EXPERTS]


def _outproj(x2d, ao, o_f, o_b, rec, rec_norm_g, w_out, norm_ffn_g, w_router, tm):
    n_tok = x2d.shape[0]
    w_bf = w_out.astype(BF16)
    wr = jnp.pad(w_router.astype(F32), ((0, 0), (0, LANES - N_EXPERTS)))
    wr_bf = wr.astype(BF16)
    row = lambda i: (i, 0)
    fixed = lambda i: (0, 0)
    return pl.pallas_call(
        _outproj_kernel,
        grid=(n_tok // tm,),
        in_specs=[
            pl.BlockSpec((tm, D_MODEL), row),
            pl.BlockSpec((tm, ATTN_W), row),
            pl.BlockSpec((tm, REC_W), row),
            pl.BlockSpec((tm, REC_W), row),
            pl.BlockSpec((tm, REC_W), lambda i: (i, 4)),
            pl.BlockSpec((1, REC_W), fixed),
            pl.BlockSpec((ATTN_W, D_MODEL), fixed),
            pl.BlockSpec((REC_W, D_MODEL), fixed),
            pl.BlockSpec((1, D_MODEL), fixed),
            pl.BlockSpec((D_MODEL, LANES), fixed),
        ],
        out_specs=[pl.BlockSpec((tm, D_MODEL), row), pl.BlockSpec((tm, D_MODEL), row),
                   pl.BlockSpec((N_EXPERTS, tm), lambda i: (0, i))],
        out_shape=[jax.ShapeDtypeStruct((n_tok, D_MODEL), F32),
                   jax.ShapeDtypeStruct((n_tok, D_MODEL), F32),
                   jax.ShapeDtypeStruct((N_EXPERTS, n_tok), F32)],
        compiler_params=_cparams(("parallel",)), name="outproj_router",
    )(x2d, ao, o_f, o_b, rec, rec_norm_g.astype(F32).reshape(1, REC_W),
      w_bf[:ATTN_W], w_bf[ATTN_W:], norm_ffn_g.astype(F32).reshape(1, D_MODEL), wr_bf)


TOPK_BLK = 256


def _topk_kernel(aff_ref, tri_ref, idx_ref, gate_ref, *, n_tok, cap):
    aff = aff_ref[...]

    def search(step, cand):
        trial = cand | (jnp.int32(1) << (30 - step))
        n_ge = jnp.sum((aff >= pltpu.bitcast(trial, F32)).astype(I32), axis=1, keepdims=True)
        return jnp.where(n_ge >= cap, trial, cand)

    thr = pltpu.bitcast(lax.fori_loop(0, 31, search, jnp.zeros((N_EXPERTS, 1), I32)), F32)
    gt = aff > thr
    tie = aff == thr
    need = (cap - jnp.sum(gt.astype(I32), axis=1, keepdims=True)).astype(F32)

    def running_count(flags):
        f = flags.astype(F32).astype(BF16)
        carry = jnp.zeros((N_EXPERTS, 1), F32)
        out = []
        for j in range(n_tok // TOPK_BLK):
            pre = _dot(f[:, j * TOPK_BLK:(j + 1) * TOPK_BLK], tri_ref[...]) + carry
            out.append(pre)
            carry = pre[:, TOPK_BLK - 1:TOPK_BLK]
        return jnp.concatenate(out, axis=1)

    sel = jnp.logical_or(gt, jnp.logical_and(tie, running_count(tie) <= need))
    slot = running_count(sel).astype(I32) - 1

    lane = lax.broadcasted_iota(I32, (N_EXPERTS, n_tok), 1)
    packed = jnp.where(sel, (slot << 16) | lane, -1)
    val = aff
    for k in range((n_tok - 1).bit_length()):
        step = 1 << k
        dist = lane - (packed >> 16)
        moving = jnp.logical_and(packed >= 0, ((dist >> k) & 1) == 1)
        arriving = pltpu.roll(jnp.where(moving, packed, -1), n_tok - step, axis=1)
        arriving_val = pltpu.roll(val, n_tok - step, axis=1)
        lands = arriving >= 0
        packed = jnp.where(lands, arriving, jnp.where(moving, -1, packed))
        val = jnp.where(lands, arriving_val, val)
    idx_ref[...] = packed[:, :cap] & 0xFFFF
    gate_ref[...] = val[:, :cap]


def _topk(aff_t, cap):
    n_tok = aff_t.shape[1]
    assert n_tok <= 32768 and n_tok % TOPK_BLK == 0, "token index and slot share one int32"
    r = np.arange(TOPK_BLK)
    tri = jnp.asarray((r[:, None] <= r[None, :]).astype(np.float32), dtype=BF16)
    return pl.pallas_call(
        functools.partial(_topk_kernel, n_tok=n_tok, cap=cap),
        out_shape=[jax.ShapeDtypeStruct((N_EXPERTS, cap), I32),
                   jax.ShapeDtypeStruct((N_EXPERTS, cap), F32)],
        compiler_params=pltpu.CompilerParams(vmem_limit_bytes=VMEM_LIMIT), name="expert_topk",
    )(aff_t, tri)


ROW_DMA_UNROLL = 8


def _row_copy(hbm, vmem, t, p, sem, gather):
    if gather:
        return pltpu.make_async_copy(hbm.at[pl.ds(t, 1), :], vmem.at[pl.ds(p, 1), :], sem)
    return pltpu.make_async_copy(vmem.at[pl.ds(p, 1), :], hbm.at[pl.ds(t, 1), :], sem)


def _rows_start_loop(hbm, vmem, idx_ref, base, cap, sem, gather):
    def issue(blk, carry):
        for u in range(ROW_DMA_UNROLL):
            p = blk * ROW_DMA_UNROLL + u
            _row_copy(hbm, vmem, idx_ref[base + p], p, sem, gather).start()
        return carry
    lax.fori_loop(0, cap // ROW_DMA_UNROLL, issue, 0)


def _rows_start_inline(hbm, vmem, idx_ref, base, p0, count, sem, gather):
    for u in range(count):
        _row_copy(hbm, vmem, idx_ref[base + p0 + u], p0 + u, sem, gather).start()


def _rows_wait(hbm, vmem, cap, sem, gather):
    if gather:
        pltpu.make_async_copy(hbm.at[pl.ds(0, cap), :], vmem, sem).wait()
    else:
        pltpu.make_async_copy(vmem, hbm.at[pl.ds(0, cap), :], sem).wait()


def _moe_kernel(idx_ref, gate_ref, fg_ref, x_hbm, acc_in_hbm, wg_ref, wu_ref, wd_ref, out_hbm,
                xg, rows, xe, ye, sem_x, sem_r, sem_s, *, cap, nf):
    del acc_in_hbm
    e = pl.program_id(0)
    f = pl.program_id(1)
    last_e = pl.num_programs(0) - 1
    chunk = cap // nf

    @pl.when(f == 0)
    def _():
        @pl.when(e == 0)
        def _():
            _rows_start_loop(x_hbm, xg, idx_ref, 0, cap, sem_x, True)
            _rows_wait(x_hbm, xg, cap, sem_x, True)

        @pl.when(e > 0)
        def _():
            _rows_wait(out_hbm, rows, cap, sem_s, False)

        xe[...] = _rms(xg[...], fg_ref[...]).astype(BF16)
        ye[...] = jnp.zeros(ye.shape, F32)

    _rows_start_inline(x_hbm, xg, idx_ref, jnp.minimum(e + 1, last_e) * cap, f * chunk, chunk,
                       sem_x, True)
    _rows_start_inline(out_hbm, rows, idx_ref, e * cap, f * chunk, chunk, sem_r, True)

    xb = xe[...]
    hg = _dot(xb, wg_ref[0].astype(BF16))
    hu = _dot(xb, wu_ref[0].astype(BF16))
    hid = hg * (1.0 / (1.0 + jnp.exp(-hg))) * hu
    ye[...] += _dot(hid.astype(BF16), wd_ref[0].astype(BF16))

    @pl.when(f == nf - 1)
    def _():
        _rows_wait(x_hbm, xg, cap, sem_x, True)
        _rows_wait(out_hbm, rows, cap, sem_r, True)
        rows[...] = rows[...] + ye[...] * gate_ref[0]
        _rows_start_inline(out_hbm, rows, idx_ref, e * cap, 0, cap, sem_s, False)

        @pl.when(e == last_e)
        def _():
            _rows_wait(out_hbm, rows, cap, sem_s, False)


def _moe(x1, x1_acc, idx, gates, norm_ffn_g, w_gate, w_up, w_down, tf):
    n_tok = x1.shape[0]
    cap = idx.shape[1]
    nf = D_EXPERT // tf
    grid_spec = pltpu.PrefetchScalarGridSpec(
        num_scalar_prefetch=1,
        grid=(N_EXPERTS, nf),
        in_specs=[
            pl.BlockSpec((1, cap, 1), lambda e, f, s: (e, 0, 0)),
            pl.BlockSpec((1, D_MODEL), lambda e, f, s: (0, 0)),
            pl.BlockSpec(memory_space=pl.ANY),
            pl.BlockSpec(memory_space=pl.ANY),
            pl.BlockSpec((1, D_MODEL, tf), lambda e, f, s: (e, 0, f)),
            pl.BlockSpec((1, D_MODEL, tf), lambda e, f, s: (e, 0, f)),
            pl.BlockSpec((1, tf, D_MODEL), lambda e, f, s: (e, f, 0)),
        ],
        out_specs=pl.BlockSpec(memory_space=pl.ANY),
        scratch_shapes=[pltpu.VMEM((cap, D_MODEL), F32), pltpu.VMEM((cap, D_MODEL), F32),
                        pltpu.VMEM((cap, D_MODEL), BF16), pltpu.VMEM((cap, D_MODEL), F32),
                        pltpu.SemaphoreType.DMA(()), pltpu.SemaphoreType.DMA(()),
                        pltpu.SemaphoreType.DMA(())],
    )
    return pl.pallas_call(
        functools.partial(_moe_kernel, cap=cap, nf=nf),
        grid_spec=grid_spec,
        out_shape=jax.ShapeDtypeStruct((n_tok, D_MODEL), F32),
        input_output_aliases={4: 0},
        compiler_params=_cparams(("arbitrary", "arbitrary")), name="expert_ffn",
    )(idx.reshape(N_EXPERTS * cap), gates.reshape(N_EXPERTS, cap, 1),
      norm_ffn_g.astype(F32).reshape(1, D_MODEL),
      x1, x1_acc, w_gate, w_up, w_down)


def _ple_kernel(x_ref, p_ref, g_ref, wg_ref, wp_ref, o_ref):
    x = x_ref[...]
    zg = _dot(_rms(x, g_ref[...]).astype(BF16), wg_ref[...])
    gate = 1.0 / (1.0 + jnp.exp(-zg))
    o_ref[...] = x + gate * _dot(p_ref[...].astype(BF16), wp_ref[...])


def _ple(x2, p2d, norm_ple_g, w_ple_gate, w_ple_proj, tm):
    n_tok = x2.shape[0]
    row = lambda i: (i, 0)
    fixed = lambda i: (0, 0)
    return pl.pallas_call(
        _ple_kernel,
        grid=(n_tok // tm,),
        in_specs=[pl.BlockSpec((tm, D_MODEL), row), pl.BlockSpec((tm, PLE_DIM), row),
                  pl.BlockSpec((1, D_MODEL), fixed), pl.BlockSpec((D_MODEL, D_MODEL), fixed),
                  pl.BlockSpec((PLE_DIM, D_MODEL), fixed)],
        out_specs=pl.BlockSpec((tm, D_MODEL), row),
        out_shape=jax.ShapeDtypeStruct((n_tok, D_MODEL), F32),
        compiler_params=_cparams(("parallel",)), name="ple_gate",
    )(x2, p2d, norm_ple_g.astype(F32).reshape(1, D_MODEL), w_ple_gate.astype(BF16),
      w_ple_proj.astype(BF16))


def _tiles(batch, seq):
    t_attn = min(512, seq)
    n_blk = batch * seq // t_attn
    t_proj = t_attn * (2 if n_blk % 2 == 0 else 1)
    t_rec = min(512, seq)
    t_row = min(256, seq)
    return t_attn, t_proj, t_rec, t_row


def _layer(x, p, norm_mix_g, w_in, q_norm_g, k_norm_g, diff_lambda, diff_norm_g, rec_lower_bound,
           rec_norm_g, w_out, norm_ffn_g, w_router, w_expert_gate, w_expert_up, w_expert_down,
           norm_ple_g, w_ple_gate, w_ple_proj):
    batch, seq, _ = x.shape
    n_tok = batch * seq
    t_attn, t_proj, t_rec, t_row = _tiles(batch, seq)
    x2d = x.reshape(n_tok, D_MODEL)
    qT, k, vT, rec = _projections(x2d, norm_mix_g[0], w_in[0], q_norm_g[0], k_norm_g[0], t_proj,
                                  t_attn)
    ao = _attention(qT, k, vT, diff_lambda[0], diff_norm_g[0], batch, seq, t_attn)
    o_f, o_b = _recurrence(rec, rec_lower_bound, batch, seq, t_rec)
    x1, x1_acc, aff = _outproj(x2d, ao, o_f, o_b, rec, rec_norm_g[0], w_out[0], norm_ffn_g[0],
                               w_router[0], t_row)
    cap = max(1, (CAPACITY_FACTOR * n_tok) // N_EXPERTS)
    idx, gates = _topk(aff, cap)
    x2 = _moe(x1, x1_acc, idx, gates, norm_ffn_g[0], w_expert_gate[0], w_expert_up[0],
              w_expert_down[0], tf=256)
    y = _ple(x2, p[0].reshape(n_tok, PLE_DIM), norm_ple_g[0], w_ple_gate[0], w_ple_proj[0], t_row)
    return y.reshape(x.shape)


def kernel(x_prompt, x_sample, p_prompt, p_sample, norm_mix_g, w_in, q_norm_g, k_norm_g, diff_lambda, diff_norm_g, rec_lower_bound, rec_norm_g, w_out, norm_ffn_g, w_router, w_expert_gate, w_expert_up, w_expert_down, norm_ple_g, w_ple_gate, w_ple_proj):
    weights = (norm_mix_g, w_in, q_norm_g, k_norm_g, diff_lambda, diff_norm_g, rec_lower_bound,
               rec_norm_g, w_out, norm_ffn_g, w_router, w_expert_gate, w_expert_up, w_expert_down,
               norm_ple_g, w_ple_gate, w_ple_proj)
    return (_layer(x_prompt, p_prompt, *weights), _layer(x_sample, p_sample, *weights))
```

```python
import functools
import math

import numpy as np
import jax
import jax.numpy as jnp
from jax import lax
from jax.experimental import pallas as pl
from jax.experimental.pallas import tpu as pltpu

F32 = jnp.float32
BF16 = jnp.bfloat16
I32 = jnp.int32

D_MODEL = 2048
PLE_DIM = 256
HEADS = 8
QK_DIM = 64
V_DIM = 128
HEAD_W = 128
ATTN_W = HEADS * V_DIM
REC_W = HEADS * V_DIM
N_EXPERTS = 16
CAPACITY_FACTOR = 2
D_EXPERT = 2048
NORM_EPS = 1e-6
LAM_INIT = 0.8 - 0.6 * math.exp(-0.3 * 0)

LANES = 128
LOG2E = 1.4426950408889634
VMEM_LIMIT = 56 * 1024 * 1024

REC_CHUNK = 64
REC_LEVELS = (1, 2, 4, 8, 16, 32)


def _cparams(sem, vmem=VMEM_LIMIT):
    return pltpu.CompilerParams(dimension_semantics=sem, vmem_limit_bytes=vmem)


def _dot(a, b):
    return jnp.dot(a, b, preferred_element_type=F32)


def _dot_nt(a, b):
    return lax.dot_general(a, b, (((1,), (1,)), ((), ())), preferred_element_type=F32)


def _dot_tn(a, b):
    return lax.dot_general(a, b, (((0,), (0,)), ((), ())), preferred_element_type=F32)


def _rms(x, g):
    ms = jnp.mean(x * x, axis=-1, keepdims=True)
    return x * lax.rsqrt(ms + NORM_EPS) * g


PROJ_TN = 512
QK_GROUP_TILE = 256
Q_BLKS = ATTN_W // PROJ_TN
REC_SEGS = 5


def _proj_kernel(x_ref, g_ref, w_ref, qg_ref, kg_ref, bd_ref, qT_ref, k_ref, vT_ref, rec_ref, h_ref,
                 *, tq):
    j = pl.program_id(1)

    @pl.when(j == 0)
    def _():
        h_ref[...] = _rms(x_ref[...], g_ref[...]).astype(BF16)

    tm = h_ref.shape[0]

    def project():
        return _dot(h_ref[...], w_ref[...])

    def qk_norm(gain_row):
        acc = project()
        parts = []
        for c0 in range(0, PROJ_TN, QK_GROUP_TILE):
            sub = acc[:, c0:c0 + QK_GROUP_TILE]
            ss = _dot((sub * sub).astype(BF16), bd_ref[...])
            parts.append(sub * lax.rsqrt(ss * (1.0 / QK_DIM) + NORM_EPS))
        return jnp.concatenate(parts, axis=1) * gain_row

    def store_transposed(o_ref, val):
        for r in range(tm // tq):
            o_ref[r] = val[r * tq:(r + 1) * tq, :].T.astype(o_ref.dtype)

    @pl.when(j < Q_BLKS)
    def _():
        store_transposed(qT_ref, qk_norm(qg_ref[...]))

    @pl.when(jnp.logical_and(j >= Q_BLKS, j < 2 * Q_BLKS))
    def _():
        k_ref[...] = qk_norm(kg_ref[...]).astype(k_ref.dtype)

    @pl.when(jnp.logical_and(j >= 2 * Q_BLKS, j < 3 * Q_BLKS))
    def _():
        store_transposed(vT_ref, project())

    @pl.when(j >= 3 * Q_BLKS)
    def _():
        acc = project()
        seg = (j - 3 * Q_BLKS) // Q_BLKS
        is_silu = jnp.logical_or(seg == 0, seg == REC_SEGS - 1)
        rec_ref[...] = jnp.where(is_silu, acc * (1.0 / (1.0 + jnp.exp(-acc))), acc)


def _projections(x2d, norm_g, w_in, q_norm_g, k_norm_g, tm, tq):
    n_tok = x2d.shape[0]
    qg_row = jnp.tile(q_norm_g.astype(F32) * (QK_DIM ** -0.5 * LOG2E), 2 * HEADS).reshape(1, ATTN_W)
    kg_row = jnp.tile(k_norm_g.astype(F32), 2 * HEADS).reshape(1, ATTN_W)
    grp = np.arange(QK_GROUP_TILE) // QK_DIM
    bd = jnp.asarray((grp[:, None] == grp[None, :]).astype(np.float32), dtype=BF16)
    n_col_blocks = (3 * ATTN_W + REC_SEGS * REC_W) // PROJ_TN

    def seg_block(first):
        def index(j, n_blocks):
            return jnp.clip(j - first, 0, n_blocks - 1)
        return index

    q_blk, k_blk, v_blk, r_blk = (seg_block(0), seg_block(Q_BLKS), seg_block(2 * Q_BLKS),
                                  seg_block(3 * Q_BLKS))
    t_shape = (tm // tq, PROJ_TN, tq)
    return pl.pallas_call(
        functools.partial(_proj_kernel, tq=tq),
        grid=(n_tok // tm, n_col_blocks),
        in_specs=[
            pl.BlockSpec((tm, D_MODEL), lambda i, j: (i, 0)),
            pl.BlockSpec((1, D_MODEL), lambda i, j: (0, 0)),
            pl.BlockSpec((D_MODEL, PROJ_TN), lambda i, j: (0, j)),
            pl.BlockSpec((1, PROJ_TN), lambda i, j: (0, q_blk(j, Q_BLKS))),
            pl.BlockSpec((1, PROJ_TN), lambda i, j: (0, k_blk(j, Q_BLKS))),
            pl.BlockSpec((QK_GROUP_TILE, QK_GROUP_TILE), lambda i, j: (0, 0)),
        ],
        out_specs=[
            pl.BlockSpec(t_shape, lambda i, j: (i, q_blk(j, Q_BLKS), 0)),
            pl.BlockSpec((tm, PROJ_TN), lambda i, j: (i, k_blk(j, Q_BLKS))),
            pl.BlockSpec(t_shape, lambda i, j: (i, v_blk(j, Q_BLKS), 0)),
            pl.BlockSpec((tm, PROJ_TN), lambda i, j: (i, r_blk(j, REC_SEGS * Q_BLKS))),
        ],
        out_shape=[
            jax.ShapeDtypeStruct((n_tok // tq, ATTN_W, tq), BF16),
            jax.ShapeDtypeStruct((n_tok, ATTN_W), BF16),
            jax.ShapeDtypeStruct((n_tok // tq, ATTN_W, tq), BF16),
            jax.ShapeDtypeStruct((n_tok, REC_SEGS * REC_W), F32),
        ],
        scratch_shapes=[pltpu.VMEM((tm, D_MODEL), BF16)],
        compiler_params=_cparams(("parallel", "arbitrary")), name="in_proj",
    )(x2d, norm_g.reshape(1, D_MODEL), w_in.astype(BF16), qg_row, kg_row, bd)


L_ROWS = 16
SKIP_LOG2_MARGIN = 160.0
SKIP_NORM_SLACK = 1.02


def _attn_kernel(slopes_ref, qT_ref, k_ref, vT_ref, lam_ref, g_ref, o_ref,
                 m_ref, acc_ref, bias_ref, ta_ref, tb_ref, mta_ref, mtb_ref, kn_ref, *, nk, tk, tq):
    h = pl.program_id(1)
    i = pl.program_id(2)
    sl = slopes_ref[h]
    lane_h = lax.broadcasted_iota(I32, (1, HEAD_W), 1)
    lane_v = lax.broadcasted_iota(I32, (1, LANES), 1)

    @pl.when(i == 0)
    def _():
        rel = (lax.broadcasted_iota(I32, (tk, tq), 1)
               - lax.broadcasted_iota(I32, (tk, tq), 0)).astype(F32)
        sr = sl * rel
        sa = sl * jnp.abs(rel)
        bias_ref[0] = jnp.concatenate([sr, sr], axis=1)
        bias_ref[1] = jnp.concatenate([sa, sa], axis=1)
        bias_ref[2] = jnp.concatenate([-sr, -sr], axis=1)

        def block_norm(j, carry):
            kf = k_ref[pl.ds(pl.multiple_of(j * tk, tk), tk), :].astype(F32)
            sq = kf * kf
            out = []
            for c in range(2):
                in_map = (lane_h >= c * QK_DIM) & (lane_h < (c + 1) * QK_DIM)
                n2 = jnp.sum(jnp.where(in_map, sq, 0.0), axis=1, keepdims=True)
                nmax = jnp.sqrt(jnp.max(n2, axis=0, keepdims=True))
                out.append(jnp.where(lane_v == j, nmax, carry[c]))
            return tuple(out)

        kn0, kn1 = lax.fori_loop(0, nk, block_norm,
                                 (jnp.zeros((1, LANES), F32), jnp.zeros((1, LANES), F32)))
        kn_ref[0:1, :] = kn0
        kn_ref[1:2, :] = kn1

    qT = qT_ref[0]
    zero_half = jnp.zeros((QK_DIM, tq), qT.dtype)
    q_cat = jnp.concatenate([jnp.concatenate([qT[:QK_DIM], zero_half], axis=0),
                             jnp.concatenate([zero_half, qT[QK_DIM:]], axis=0)], axis=1)
    ones = jnp.ones((L_ROWS, tk), BF16)
    m_ref[...] = jnp.full(m_ref.shape, -jnp.inf, F32)
    acc_ref[...] = jnp.zeros(acc_ref.shape, F32)

    def shift_of(j):
        gap = jnp.abs(jnp.full((1, 2 * tq), (i - j) * tq, I32)).astype(F32)
        return -sl * gap

    def scores(j, t_ref, mt_ref):
        kblk = k_ref[pl.ds(pl.multiple_of(j * tk, tk), tk), :]
        side = jnp.where(j < i, 0, jnp.where(j == i, 1, 2))
        t = _dot(kblk, q_cat) - bias_ref[side]
        t_ref[...] = t
        mt_ref[...] = jnp.max(t, axis=0, keepdims=True) + shift_of(j)

    def accumulate(j, t_ref, mt_ref):
        m_old = m_ref[...]
        m_new = jnp.maximum(m_old, mt_ref[...])
        p = jnp.exp2(t_ref[...] - (m_new - shift_of(j)))
        v_aug = jnp.concatenate([vT_ref[j], ones], axis=0)
        acc_ref[...] = jnp.exp2(m_old - m_new) * acc_ref[...] + _dot(v_aug, p.astype(BF16))
        m_ref[...] = m_new

    scores(i, ta_ref, mta_ref)

    qf = qT.astype(F32)
    qsq = qf * qf
    thr = None
    for c in range(2):
        rows_c = slice(c * QK_DIM, (c + 1) * QK_DIM)
        bq = jnp.sqrt(jnp.max(jnp.sum(qsq[rows_c], axis=0, keepdims=True), axis=1, keepdims=True))
        kn = kn_ref[c:c + 1, :]
        k_all = jnp.max(kn, axis=1, keepdims=True)
        k_diag = jnp.max(jnp.where(lane_v == i, kn, 0.0), axis=1, keepdims=True)
        thr_c = SKIP_NORM_SLACK * bq * (k_all + k_diag) + SKIP_LOG2_MARGIN
        thr = thr_c if thr is None else jnp.maximum(thr, thr_c)
    dist = lane_v.astype(F32)
    visit = (lane_v >= 1) & (sl * ((dist - 1.0) * tq + 1.0) <= thr)
    radius = jnp.sum(visit.astype(I32))
    lo = jnp.maximum(i - radius, 0)
    hi = jnp.minimum(i + radius, nk - 1)
    n_vis = hi - lo + 1

    def visited(n):
        j = lo + n - 1
        return jnp.where(n == 0, i, jnp.where(j < i, j, j + 1))

    def pair(jj, carry):
        n = 2 * jj
        scores(visited(n + 1), tb_ref, mtb_ref)
        accumulate(visited(n), ta_ref, mta_ref)
        scores(visited(n + 2), ta_ref, mta_ref)
        accumulate(visited(n + 1), tb_ref, mtb_ref)
        return carry

    lax.fori_loop(0, jnp.right_shift(n_vis - 1, 1), pair, 0)

    @pl.when(jnp.bitwise_and(n_vis, 1) == 1)
    def _():
        accumulate(visited(n_vis - 1), ta_ref, mta_ref)

    @pl.when(jnp.bitwise_and(n_vis, 1) == 0)
    def _():
        scores(visited(n_vis - 1), tb_ref, mtb_ref)
        accumulate(visited(n_vis - 2), ta_ref, mta_ref)
        accumulate(visited(n_vis - 1), tb_ref, mtb_ref)

    lp = lam_ref[...]
    lam = (jnp.exp(jnp.sum(lp[0:1] * lp[1:2], axis=-1, keepdims=True))
           - jnp.exp(jnp.sum(lp[2:3] * lp[3:4], axis=-1, keepdims=True)) + LAM_INIT)
    acc = acc_ref[...]
    a0 = acc[:, :tq]
    a1 = acc[:, tq:]
    o = (a0[:V_DIM] / a0[V_DIM:V_DIM + 1]
         - lam * (a1[:V_DIM] / a1[V_DIM:V_DIM + 1]))
    ms = jnp.mean(o * o, axis=0, keepdims=True)
    on = o * lax.rsqrt(ms + NORM_EPS) * g_ref[...] * (1.0 - LAM_INIT)
    o_ref[...] = on.T.astype(o_ref.dtype)


def _attention(qT, k, vT, diff_lambda, diff_norm_g, batch, seq, tq):
    nq = seq // tq
    assert nq <= LANES, "per-block key norms are kept one per lane"
    slopes = LOG2E * jnp.exp2(-8.0 * jnp.arange(1, HEADS + 1, dtype=F32) / HEADS)
    g_col = diff_norm_g.astype(F32).reshape(ATTN_W, 1)
    grid_spec = pltpu.PrefetchScalarGridSpec(
        num_scalar_prefetch=1,
        grid=(batch, HEADS, nq),
        in_specs=[
            pl.BlockSpec((1, HEAD_W, tq), lambda b, h, i, s: (b * nq + i, h, 0)),
            pl.BlockSpec((seq, HEAD_W), lambda b, h, i, s: (b, h)),
            pl.BlockSpec((nq, HEAD_W, tq), lambda b, h, i, s: (b, h, 0)),
            pl.BlockSpec((4, QK_DIM), lambda b, h, i, s: (0, 0)),
            pl.BlockSpec((HEAD_W, 1), lambda b, h, i, s: (h, 0)),
        ],
        out_specs=pl.BlockSpec((tq, HEAD_W), lambda b, h, i, s: (b * nq + i, h)),
        scratch_shapes=[pltpu.VMEM((1, 2 * tq), F32),
                        pltpu.VMEM((V_DIM + L_ROWS, 2 * tq), F32),
                        pltpu.VMEM((3, tq, 2 * tq), F32),
                        pltpu.VMEM((tq, 2 * tq), F32), pltpu.VMEM((tq, 2 * tq), F32),
                        pltpu.VMEM((1, 2 * tq), F32), pltpu.VMEM((1, 2 * tq), F32),
                        pltpu.VMEM((2, LANES), F32)],
    )
    return pl.pallas_call(
        functools.partial(_attn_kernel, nk=nq, tk=tq, tq=tq),
        grid_spec=grid_spec,
        out_shape=jax.ShapeDtypeStruct((batch * seq, ATTN_W), BF16),
        compiler_params=_cparams(("arbitrary", "arbitrary", "arbitrary")), name="diff_attn",
    )(slopes, qT, k, vT, diff_lambda.astype(F32), g_col)


SUBLANES = 8
REC_MAT_LEVELS = tuple(lv for lv in REC_LEVELS if 4 * lv < SUBLANES)


def _rec_boundary_row(lv, group, reverse):
    return group * 2 * lv + (lv if reverse else lv - 1)


def _rec_mats(reverse):
    c = REC_CHUNK
    t = np.arange(c)[:, None]
    u = np.arange(c)[None, :]
    mats = [(u >= t) if reverse else (u <= t)]
    for lv in REC_MAT_LEVELS:
        bd = _rec_boundary_row(lv, t // (2 * lv), reverse)
        mats.append((u >= bd) if reverse else (u <= bd))
    return jnp.asarray(np.concatenate(mats, axis=0).astype(np.float32), dtype=BF16)


def _rec_gates(q_ref, z_ref, v_ref, lbp_ref, mats_ref, n_chunks):
    c = REC_CHUNK
    lbp = lbp_ref[...]
    mx = jnp.maximum(lbp[0:1], lbp[1:2])
    e0 = jnp.exp(lbp[0:1] - mx)
    e1 = jnp.exp(lbp[1:2] - mx)
    lb = e0 / (e0 + e1)

    rows = [slice(n * c, (n + 1) * c) for n in range(n_chunks)]
    qs = [q_ref[sl, :] for sl in rows]
    vs = [v_ref[sl, :].astype(BF16) for sl in rows]
    gs, kks = [], []
    for sl in rows:
        z = z_ref[sl, :]
        sig = 1.0 / (1.0 + jnp.exp(-z))
        gs.append(jnp.log2(lb + (1.0 - lb) * sig))
        kks.append((1.0 - lb) * (1.0 / (1.0 + jnp.exp(z))))

    g_cat = jnp.concatenate(gs, axis=1)
    g1 = g_cat.astype(BF16)
    r1 = g_cat - g1.astype(F32)
    g2 = r1.astype(BF16)
    g3 = (r1 - g2.astype(F32)).astype(BF16)
    mats = mats_ref[...]
    stacked = (_dot(mats, g3) + _dot(mats, g2)) + _dot(mats, g1)
    return qs, kks, vs, stacked


def _rec_scores(qs, kks, vs, stacked, reverse):
    c = REC_CHUNK
    n_chunks = len(qs)
    row = lax.broadcasted_iota(I32, (c, HEAD_W), 0)
    ti = lax.broadcasted_iota(I32, (c, c), 0)
    si = lax.broadcasted_iota(I32, (c, c), 1)
    is_query, same = [], []
    for lv in REC_LEVELS:
        in_upper = (row & (2 * lv - 1)) >= lv
        is_query.append(jnp.logical_not(in_upper) if reverse else in_upper)
        shift = (2 * lv).bit_length() - 1
        same.append((ti >> shift) == (si >> shift))
    diag = ti == si

    o_intra, incs, qes, decays = [], [], [], []
    for n in range(n_chunks):
        lanes = slice(n * HEAD_W, (n + 1) * HEAD_W)
        b = stacked[0:c, lanes]
        q, kk = qs[n], kks[n]
        q_bf, kk_bf = q.astype(BF16), kk.astype(BF16)
        a = jnp.where(diag, _dot_nt(q_bf, kk_bf), 0.0)
        for m, lv in enumerate(REC_LEVELS):
            if lv in REC_MAT_LEVELS:
                mi = REC_MAT_LEVELS.index(lv)
                r = stacked[(mi + 1) * c:(mi + 2) * c, lanes]
            else:
                r = jnp.concatenate(
                    [jnp.broadcast_to(b[bd:bd + 1], (2 * lv, HEAD_W))
                     for bd in (_rec_boundary_row(lv, grp, reverse) for grp in range(c // (2 * lv)))],
                    axis=0)
            e = jnp.exp2(-jnp.abs(b - r))
            qt = jnp.where(is_query[m], q * e, 0.0).astype(BF16)
            kt = jnp.where(is_query[m], 0.0, kk * e).astype(BF16)
            part = _dot_nt(qt, kt)
            a = a + (part if 2 * lv == c else jnp.where(same[m], part, 0.0))
        b_end = b[0:1] if reverse else b[c - 1:c]
        o_intra.append(_dot(a.astype(BF16), vs[n]))
        incs.append(_dot_tn(vs[n], (kk * jnp.exp2(b_end - b)).astype(BF16)))
        qes.append((q * jnp.exp2(b)).astype(BF16))
        decays.append(jnp.exp2(b_end))
    return o_intra, incs, qes, decays


def _rec_scan(o_intra, incs, qes, decays, o_ref, st_ref, reverse):
    n_chunks = len(o_intra)
    st = st_ref[...]
    order = range(n_chunks - 1, -1, -1) if reverse else range(n_chunks)
    for n in order:
        o_ref[n * REC_CHUNK:(n + 1) * REC_CHUNK, :] = o_intra[n] + _dot_nt(qes[n], st.astype(BF16))
        st = st * decays[n] + incs[n]
    st_ref[...] = st


def _rec_kernel(qf_ref, zf_ref, vf_ref, qb_ref, zb_ref, vb_ref, lbpf_ref, lbpb_ref, matsf_ref,
                matsb_ref, of_ref, ob_ref, stf_ref, stb_ref, *, n_chunks):
    @pl.when(pl.program_id(2) == 0)
    def _():
        stf_ref[...] = jnp.zeros(stf_ref.shape, F32)
        stb_ref[...] = jnp.zeros(stb_ref.shape, F32)

    gates_f = _rec_gates(qf_ref, zf_ref, vf_ref, lbpf_ref, matsf_ref, n_chunks)
    gates_b = _rec_gates(qb_ref, zb_ref, vb_ref, lbpb_ref, matsb_ref, n_chunks)
    parts_f = _rec_scores(*gates_f, False)
    parts_b = _rec_scores(*gates_b, True)
    _rec_scan(*parts_f, of_ref, stf_ref, False)
    _rec_scan(*parts_b, ob_ref, stb_ref, True)


def _recurrence(rec, rec_lower_bound, batch, seq, tb):
    nb = seq // tb
    lbp = rec_lower_bound.astype(F32)

    def fwd(col):
        return lambda b, h, i: (b * nb + i, col + h)

    def bwd(col):
        return lambda b, h, i: (b * nb + nb - 1 - i, col + h)

    blk = (tb, HEAD_W)
    mat_shape = ((len(REC_MAT_LEVELS) + 1) * REC_CHUNK, REC_CHUNK)
    out = jax.ShapeDtypeStruct((batch * seq, REC_W), F32)
    return pl.pallas_call(
        functools.partial(_rec_kernel, n_chunks=tb // REC_CHUNK),
        grid=(batch, HEADS, nb),
        in_specs=[
            pl.BlockSpec(blk, fwd(0)), pl.BlockSpec(blk, fwd(HEADS)), pl.BlockSpec(blk, fwd(3 * HEADS)),
            pl.BlockSpec(blk, bwd(0)), pl.BlockSpec(blk, bwd(2 * HEADS)), pl.BlockSpec(blk, bwd(3 * HEADS)),
            pl.BlockSpec((2, HEAD_W), lambda b, h, i: (0, h)),
            pl.BlockSpec((2, HEAD_W), lambda b, h, i: (0, h)),
            pl.BlockSpec(mat_shape, lambda b, h, i: (0, 0)),
            pl.BlockSpec(mat_shape, lambda b, h, i: (0, 0)),
        ],
        out_specs=[pl.BlockSpec(blk, fwd(0)), pl.BlockSpec(blk, bwd(0))],
        out_shape=[out, out],
        scratch_shapes=[pltpu.VMEM((HEAD_W, HEAD_W), F32), pltpu.VMEM((HEAD_W, HEAD_W), F32)],
        compiler_params=_cparams(("parallel", "parallel", "arbitrary")), name="rec_bidir",
    )(rec, rec, rec, rec, rec, rec, lbp[0], lbp[1], _rec_mats(False), _rec_mats(True))


def _split_bf16(x):
    hi = x.astype(BF16)
    lo = (x - hi.astype(F32)).astype(BF16)
    return hi, lo


def _outproj_kernel(x_ref, ao_ref, of_ref, ob_ref, rg_ref, rng_ref, woa_ref, wor_ref,
                    fg_ref, wr_ref, x1_ref, x1acc_ref, aff_ref):
    parts = []
    for hd in range(HEADS):
        sl = slice(hd * HEAD_W, (hd + 1) * HEAD_W)
        o = of_ref[:, sl] + ob_ref[:, sl]
        parts.append((_rms(o, rng_ref[:, sl]) * rg_ref[:, sl]).astype(BF16))
    ro = jnp.concatenate(parts, axis=-1)
    x1 = x_ref[...] + _dot(ao_ref[...], woa_ref[...]) + _dot(ro, wor_ref[...])
    x1_ref[...] = x1
    x1acc_ref[...] = x1
    h2 = _rms(x1, fg_ref[...])
    hi, lo = _split_bf16(h2)
    logits = _dot(hi, wr_ref[...]) + _dot(lo, wr_ref[...])
    lane = lax.broadcasted_iota(I32, logits.shape, 1)
    logits = jnp.where(lane < N_EXPERTS, logits, -jnp.inf)
    mx = jnp.max(logits, axis=-1, keepdims=True)
    ex = jnp.exp(logits - mx)
    aff = ex / jnp.sum(ex, axis=-1, keepdims=True)
    aff_ref[...] = aff.T[:N_EXPERTS]


def _outproj(x2d, ao, o_f, o_b, rec, rec_norm_g, w_out, norm_ffn_g, w_router, tm):
    n_tok = x2d.shape[0]
    w_bf = w_out.astype(BF16)
    wr = jnp.pad(w_router.astype(F32), ((0, 0), (0, LANES - N_EXPERTS)))
    wr_bf = wr.astype(BF16)
    row = lambda i: (i, 0)
    fixed = lambda i: (0, 0)
    return pl.pallas_call(
        _outproj_kernel,
        grid=(n_tok // tm,),
        in_specs=[
            pl.BlockSpec((tm, D_MODEL), row),
            pl.BlockSpec((tm, ATTN_W), row),
            pl.BlockSpec((tm, REC_W), row),
            pl.BlockSpec((tm, REC_W), row),
            pl.BlockSpec((tm, REC_W), lambda i: (i, 4)),
            pl.BlockSpec((1, REC_W), fixed),
            pl.BlockSpec((ATTN_W, D_MODEL), fixed),
            pl.BlockSpec((REC_W, D_MODEL), fixed),
            pl.BlockSpec((1, D_MODEL), fixed),
            pl.BlockSpec((D_MODEL, LANES), fixed),
        ],
        out_specs=[pl.BlockSpec((tm, D_MODEL), row), pl.BlockSpec((tm, D_MODEL), row),
                   pl.BlockSpec((N_EXPERTS, tm), lambda i: (0, i))],
        out_shape=[jax.ShapeDtypeStruct((n_tok, D_MODEL), F32),
                   jax.ShapeDtypeStruct((n_tok, D_MODEL), F32),
                   jax.ShapeDtypeStruct((N_EXPERTS, n_tok), F32)],
        compiler_params=_cparams(("parallel",)), name="outproj_router",
    )(x2d, ao, o_f, o_b, rec, rec_norm_g.astype(F32).reshape(1, REC_W),
      w_bf[:ATTN_W], w_bf[ATTN_W:], norm_ffn_g.astype(F32).reshape(1, D_MODEL), wr_bf)


TOPK_BLK = 256


def _topk_kernel(aff_ref, tri_ref, idx_ref, gate_ref, *, n_tok, cap):
    aff = aff_ref[...]

    def search(step, cand):
        trial = cand | (jnp.int32(1) << (30 - step))
        n_ge = jnp.sum((aff >= pltpu.bitcast(trial, F32)).astype(I32), axis=1, keepdims=True)
        return jnp.where(n_ge >= cap, trial, cand)

    thr = pltpu.bitcast(lax.fori_loop(0, 31, search, jnp.zeros((N_EXPERTS, 1), I32)), F32)
    gt = aff > thr
    tie = aff == thr
    need = (cap - jnp.sum(gt.astype(I32), axis=1, keepdims=True)).astype(F32)

    def running_count(flags):
        f = flags.astype(F32).astype(BF16)
        carry = jnp.zeros((N_EXPERTS, 1), F32)
        out = []
        for j in range(n_tok // TOPK_BLK):
            pre = _dot(f[:, j * TOPK_BLK:(j + 1) * TOPK_BLK], tri_ref[...]) + carry
            out.append(pre)
            carry = pre[:, TOPK_BLK - 1:TOPK_BLK]
        return jnp.concatenate(out, axis=1)

    sel = jnp.logical_or(gt, jnp.logical_and(tie, running_count(tie) <= need))
    slot = running_count(sel).astype(I32) - 1

    lane = lax.broadcasted_iota(I32, (N_EXPERTS, n_tok), 1)
    packed = jnp.where(sel, (slot << 16) | lane, -1)
    val = aff
    for k in range((n_tok - 1).bit_length()):
        step = 1 << k
        dist = lane - (packed >> 16)
        moving = jnp.logical_and(packed >= 0, ((dist >> k) & 1) == 1)
        arriving = pltpu.roll(jnp.where(moving, packed, -1), n_tok - step, axis=1)
        arriving_val = pltpu.roll(val, n_tok - step, axis=1)
        lands = arriving >= 0
        packed = jnp.where(lands, arriving, jnp.where(moving, -1, packed))
        val = jnp.where(lands, arriving_val, val)
    idx_ref[...] = packed[:, :cap] & 0xFFFF
    gate_ref[...] = val[:, :cap]


def _topk(aff_t, cap):
    n_tok = aff_t.shape[1]
    assert n_tok <= 32768 and n_tok % TOPK_BLK == 0, "token index and slot share one int32"
    r = np.arange(TOPK_BLK)
    tri = jnp.asarray((r[:, None] <= r[None, :]).astype(np.float32), dtype=BF16)
    return pl.pallas_call(
        functools.partial(_topk_kernel, n_tok=n_tok, cap=cap),
        out_shape=[jax.ShapeDtypeStruct((N_EXPERTS, cap), I32),
                   jax.ShapeDtypeStruct((N_EXPERTS, cap), F32)],
        compiler_params=pltpu.CompilerParams(vmem_limit_bytes=VMEM_LIMIT), name="expert_topk",
    )(aff_t, tri)


ROW_DMA_UNROLL = 8


def _row_copy(hbm, vmem, t, p, sem, gather):
    if gather:
        return pltpu.make_async_copy(hbm.at[pl.ds(t, 1), :], vmem.at[pl.ds(p, 1), :], sem)
    return pltpu.make_async_copy(vmem.at[pl.ds(p, 1), :], hbm.at[pl.ds(t, 1), :], sem)


def _rows_start_loop(hbm, vmem, idx_ref, base, cap, sem, gather):
    def issue(blk, carry):
        for u in range(ROW_DMA_UNROLL):
            p = blk * ROW_DMA_UNROLL + u
            _row_copy(hbm, vmem, idx_ref[base + p], p, sem, gather).start()
        return carry
    lax.fori_loop(0, cap // ROW_DMA_UNROLL, issue, 0)


def _rows_start_inline(hbm, vmem, idx_ref, base, p0, count, sem, gather):
    for u in range(count):
        _row_copy(hbm, vmem, idx_ref[base + p0 + u], p0 + u, sem, gather).start()


def _rows_wait(hbm, vmem, cap, sem, gather):
    if gather:
        pltpu.make_async_copy(hbm.at[pl.ds(0, cap), :], vmem, sem).wait()
    else:
        pltpu.make_async_copy(vmem, hbm.at[pl.ds(0, cap), :], sem).wait()


def _moe_kernel(idx_ref, gate_ref, fg_ref, x_hbm, acc_in_hbm, wg_ref, wu_ref, wd_ref, out_hbm,
                xg, rows, xe, ye, sem_x, sem_r, sem_s, *, cap, nf):
    del acc_in_hbm
    e = pl.program_id(0)
    f = pl.program_id(1)
    last_e = pl.num_programs(0) - 1
    chunk = cap // nf

    @pl.when(f == 0)
    def _():
        @pl.when(e == 0)
        def _():
            _rows_start_loop(x_hbm, xg, idx_ref, 0, cap, sem_x, True)
            _rows_wait(x_hbm, xg, cap, sem_x, True)

        @pl.when(e > 0)
        def _():
            _rows_wait(out_hbm, rows, cap, sem_s, False)

        xe[...] = _rms(xg[...], fg_ref[...]).astype(BF16)
        ye[...] = jnp.zeros(ye.shape, F32)

    _rows_start_inline(x_hbm, xg, idx_ref, jnp.minimum(e + 1, last_e) * cap, f * chunk, chunk,
                       sem_x, True)
    _rows_start_inline(out_hbm, rows, idx_ref, e * cap, f * chunk, chunk, sem_r, True)

    xb = xe[...]
    hg = _dot(xb, wg_ref[0].astype(BF16))
    hu = _dot(xb, wu_ref[0].astype(BF16))
    hid = hg * (1.0 / (1.0 + jnp.exp(-hg))) * hu
    ye[...] += _dot(hid.astype(BF16), wd_ref[0].astype(BF16))

    @pl.when(f == nf - 1)
    def _():
        _rows_wait(x_hbm, xg, cap, sem_x, True)
        _rows_wait(out_hbm, rows, cap, sem_r, True)
        rows[...] = rows[...] + ye[...] * gate_ref[0]
        _rows_start_inline(out_hbm, rows, idx_ref, e * cap, 0, cap, sem_s, False)

        @pl.when(e == last_e)
        def _():
            _rows_wait(out_hbm, rows, cap, sem_s, False)


def _moe(x1, x1_acc, idx, gates, norm_ffn_g, w_gate, w_up, w_down, tf):
    n_tok = x1.shape[0]
    cap = idx.shape[1]
    nf = D_EXPERT // tf
    grid_spec = pltpu.PrefetchScalarGridSpec(
        num_scalar_prefetch=1,
        grid=(N_EXPERTS, nf),
        in_specs=[
            pl.BlockSpec((1, cap, 1), lambda e, f, s: (e, 0, 0)),
            pl.BlockSpec((1, D_MODEL), lambda e, f, s: (0, 0)),
            pl.BlockSpec(memory_space=pl.ANY),
            pl.BlockSpec(memory_space=pl.ANY),
            pl.BlockSpec((1, D_MODEL, tf), lambda e, f, s: (e, 0, f)),
            pl.BlockSpec((1, D_MODEL, tf), lambda e, f, s: (e, 0, f)),
            pl.BlockSpec((1, tf, D_MODEL), lambda e, f, s: (e, f, 0)),
        ],
        out_specs=pl.BlockSpec(memory_space=pl.ANY),
        scratch_shapes=[pltpu.VMEM((cap, D_MODEL), F32), pltpu.VMEM((cap, D_MODEL), F32),
                        pltpu.VMEM((cap, D_MODEL), BF16), pltpu.VMEM((cap, D_MODEL), F32),
                        pltpu.SemaphoreType.DMA(()), pltpu.SemaphoreType.DMA(()),
                        pltpu.SemaphoreType.DMA(())],
    )
    return pl.pallas_call(
        functools.partial(_moe_kernel, cap=cap, nf=nf),
        grid_spec=grid_spec,
        out_shape=jax.ShapeDtypeStruct((n_tok, D_MODEL), F32),
        input_output_aliases={4: 0},
        compiler_params=_cparams(("arbitrary", "arbitrary")), name="expert_ffn",
    )(idx.reshape(N_EXPERTS * cap), gates.reshape(N_EXPERTS, cap, 1),
      norm_ffn_g.astype(F32).reshape(1, D_MODEL),
      x1, x1_acc, w_gate, w_up, w_down)


def _ple_kernel(x_ref, p_ref, g_ref, wg_ref, wp_ref, o_ref):
    x = x_ref[...]
    zg = _dot(_rms(x, g_ref[...]).astype(BF16), wg_ref[...])
    gate = 1.0 / (1.0 + jnp.exp(-zg))
    o_ref[...] = x + gate * _dot(p_ref[...].astype(BF16), wp_ref[...])


def _ple(x2, p2d, norm_ple_g, w_ple_gate, w_ple_proj, tm):
    n_tok = x2.shape[0]
    row = lambda i: (i, 0)
    fixed = lambda i: (0, 0)
    return pl.pallas_call(
        _ple_kernel,
        grid=(n_tok // tm,),
        in_specs=[pl.BlockSpec((tm, D_MODEL), row), pl.BlockSpec((tm, PLE_DIM), row),
                  pl.BlockSpec((1, D_MODEL), fixed), pl.BlockSpec((D_MODEL, D_MODEL), fixed),
                  pl.BlockSpec((PLE_DIM, D_MODEL), fixed)],
        out_specs=pl.BlockSpec((tm, D_MODEL), row),
        out_shape=jax.ShapeDtypeStruct((n_tok, D_MODEL), F32),
        compiler_params=_cparams(("parallel",)), name="ple_gate",
    )(x2, p2d, norm_ple_g.astype(F32).reshape(1, D_MODEL), w_ple_gate.astype(BF16),
      w_ple_proj.astype(BF16))


def _tiles(batch, seq):
    t_attn = min(512, seq)
    n_blk = batch * seq // t_attn
    t_proj = t_attn * (2 if n_blk % 2 == 0 else 1)
    t_rec = min(1024, seq)
    t_row = min(256, seq)
    return t_attn, t_proj, t_rec, t_row


def _layer(x, p, norm_mix_g, w_in, q_norm_g, k_norm_g, diff_lambda, diff_norm_g, rec_lower_bound,
           rec_norm_g, w_out, norm_ffn_g, w_router, w_expert_gate, w_expert_up, w_expert_down,
           norm_ple_g, w_ple_gate, w_ple_proj):
    batch, seq, _ = x.shape
    n_tok = batch * seq
    t_attn, t_proj, t_rec, t_row = _tiles(batch, seq)
    x2d = x.reshape(n_tok, D_MODEL)
    qT, k, vT, rec = _projections(x2d, norm_mix_g[0], w_in[0], q_norm_g[0], k_norm_g[0], t_proj,
                                  t_attn)
    ao = _attention(qT, k, vT, diff_lambda[0], diff_norm_g[0], batch, seq, t_attn)
    o_f, o_b = _recurrence(rec, rec_lower_bound, batch, seq, t_rec)
    x1, x1_acc, aff = _outproj(x2d, ao, o_f, o_b, rec, rec_norm_g[0], w_out[0], norm_ffn_g[0],
                               w_router[0], t_row)
    cap = max(1, (CAPACITY_FACTOR * n_tok) // N_EXPERTS)
    idx, gates = _topk(aff, cap)
    x2 = _moe(x1, x1_acc, idx, gates, norm_ffn_g[0], w_expert_gate[0], w_expert_up[0],
              w_expert_down[0], tf=256)
    y = _ple(x2, p[0].reshape(n_tok, PLE_DIM), norm_ple_g[0], w_ple_gate[0], w_ple_proj[0], t_row)
    return y.reshape(x.shape)


def kernel(x_prompt, x_sample, p_prompt, p_sample, norm_mix_g, w_in, q_norm_g, k_norm_g, diff_lambda, diff_norm_g, rec_lower_bound, rec_norm_g, w_out, norm_ffn_g, w_router, w_expert_gate, w_expert_up, w_expert_down, norm_ple_g, w_ple_gate, w_ple_proj):
    weights = (norm_mix_g, w_in, q_norm_g, k_norm_g, diff_lambda, diff_norm_g, rec_lower_bound,
               rec_norm_g, w_out, norm_ffn_g, w_router, w_expert_gate, w_expert_up, w_expert_down,
               norm_ple_g, w_ple_gate, w_ple_proj)
    return (_layer(x_prompt, p_prompt, *weights), _layer(x_sample, p_sample, *weights))
```

```python
import functools
import math

import numpy as np
import jax
import jax.numpy as jnp
from jax import lax
from jax.experimental import pallas as pl
from jax.experimental.pallas import tpu as pltpu

F32 = jnp.float32
BF16 = jnp.bfloat16
I32 = jnp.int32

D_MODEL = 2048
PLE_DIM = 256
HEADS = 8
QK_DIM = 64
V_DIM = 128
HEAD_W = 128
ATTN_W = HEADS * V_DIM
REC_W = HEADS * V_DIM
N_EXPERTS = 16
CAPACITY_FACTOR = 2
D_EXPERT = 2048
NORM_EPS = 1e-6
LAM_INIT = 0.8 - 0.6 * math.exp(-0.3 * 0)

LANES = 128
LOG2E = 1.4426950408889634
VMEM_LIMIT = 56 * 1024 * 1024

REC_CHUNK = 64
REC_LEVELS = (1, 2, 4, 8, 16, 32)


def _cparams(sem, vmem=VMEM_LIMIT):
    return pltpu.CompilerParams(dimension_semantics=sem, vmem_limit_bytes=vmem)


def _dot(a, b):
    return jnp.dot(a, b, preferred_element_type=F32)


def _dot_nt(a, b):
    return lax.dot_general(a, b, (((1,), (1,)), ((), ())), preferred_element_type=F32)


def _dot_tn(a, b):
    return lax.dot_general(a, b, (((0,), (0,)), ((), ())), preferred_element_type=F32)


def _rms(x, g):
    ms = jnp.mean(x * x, axis=-1, keepdims=True)
    return x * lax.rsqrt(ms + NORM_EPS) * g


PROJ_TN = 512
QK_GROUP_TILE = 256
Q_BLKS = ATTN_W // PROJ_TN
REC_SEGS = 5


def _proj_kernel(x_ref, g_ref, w_ref, qg_ref, kg_ref, bd_ref, qT_ref, k_ref, vT_ref, rec_ref, h_ref,
                 *, tq):
    j = pl.program_id(1)

    @pl.when(j == 0)
    def _():
        h_ref[...] = _rms(x_ref[...], g_ref[...]).astype(BF16)

    tm = h_ref.shape[0]

    def project():
        return _dot(h_ref[...], w_ref[...])

    def qk_norm(gain_row):
        acc = project()
        parts = []
        for c0 in range(0, PROJ_TN, QK_GROUP_TILE):
            sub = acc[:, c0:c0 + QK_GROUP_TILE]
            ss = _dot((sub * sub).astype(BF16), bd_ref[...])
            parts.append(sub * lax.rsqrt(ss * (1.0 / QK_DIM) + NORM_EPS))
        return jnp.concatenate(parts, axis=1) * gain_row

    def store_transposed(o_ref, val):
        for r in range(tm // tq):
            o_ref[r] = val[r * tq:(r + 1) * tq, :].T.astype(o_ref.dtype)

    @pl.when(j < Q_BLKS)
    def _():
        store_transposed(qT_ref, qk_norm(qg_ref[...]))

    @pl.when(jnp.logical_and(j >= Q_BLKS, j < 2 * Q_BLKS))
    def _():
        k_ref[...] = qk_norm(kg_ref[...]).astype(k_ref.dtype)

    @pl.when(jnp.logical_and(j >= 2 * Q_BLKS, j < 3 * Q_BLKS))
    def _():
        store_transposed(vT_ref, project())

    @pl.when(j >= 3 * Q_BLKS)
    def _():
        acc = project()
        seg = (j - 3 * Q_BLKS) // Q_BLKS
        is_silu = jnp.logical_or(seg == 0, seg == REC_SEGS - 1)
        rec_ref[...] = jnp.where(is_silu, acc * (1.0 / (1.0 + jnp.exp(-acc))), acc)


def _projections(x2d, norm_g, w_in, q_norm_g, k_norm_g, tm, tq):
    n_tok = x2d.shape[0]
    qg_row = jnp.tile(q_norm_g.astype(F32) * (QK_DIM ** -0.5 * LOG2E), 2 * HEADS).reshape(1, ATTN_W)
    kg_row = jnp.tile(k_norm_g.astype(F32), 2 * HEADS).reshape(1, ATTN_W)
    grp = np.arange(QK_GROUP_TILE) // QK_DIM
    bd = jnp.asarray((grp[:, None] == grp[None, :]).astype(np.float32), dtype=BF16)
    n_col_blocks = (3 * ATTN_W + REC_SEGS * REC_W) // PROJ_TN

    def seg_block(first):
        def index(j, n_blocks):
            return jnp.clip(j - first, 0, n_blocks - 1)
        return index

    q_blk, k_blk, v_blk, r_blk = (seg_block(0), seg_block(Q_BLKS), seg_block(2 * Q_BLKS),
                                  seg_block(3 * Q_BLKS))
    t_shape = (tm // tq, PROJ_TN, tq)
    return pl.pallas_call(
        functools.partial(_proj_kernel, tq=tq),
        grid=(n_tok // tm, n_col_blocks),
        in_specs=[
            pl.BlockSpec((tm, D_MODEL), lambda i, j: (i, 0)),
            pl.BlockSpec((1, D_MODEL), lambda i, j: (0, 0)),
            pl.BlockSpec((D_MODEL, PROJ_TN), lambda i, j: (0, j)),
            pl.BlockSpec((1, PROJ_TN), lambda i, j: (0, q_blk(j, Q_BLKS))),
            pl.BlockSpec((1, PROJ_TN), lambda i, j: (0, k_blk(j, Q_BLKS))),
            pl.BlockSpec((QK_GROUP_TILE, QK_GROUP_TILE), lambda i, j: (0, 0)),
        ],
        out_specs=[
            pl.BlockSpec(t_shape, lambda i, j: (i, q_blk(j, Q_BLKS), 0)),
            pl.BlockSpec((tm, PROJ_TN), lambda i, j: (i, k_blk(j, Q_BLKS))),
            pl.BlockSpec(t_shape, lambda i, j: (i, v_blk(j, Q_BLKS), 0)),
            pl.BlockSpec((tm, PROJ_TN), lambda i, j: (i, r_blk(j, REC_SEGS * Q_BLKS))),
        ],
        out_shape=[
            jax.ShapeDtypeStruct((n_tok // tq, ATTN_W, tq), BF16),
            jax.ShapeDtypeStruct((n_tok, ATTN_W), BF16),
            jax.ShapeDtypeStruct((n_tok // tq, ATTN_W, tq), BF16),
            jax.ShapeDtypeStruct((n_tok, REC_SEGS * REC_W), F32),
        ],
        scratch_shapes=[pltpu.VMEM((tm, D_MODEL), BF16)],
        compiler_params=_cparams(("parallel", "arbitrary")), name="in_proj",
    )(x2d, norm_g.reshape(1, D_MODEL), w_in.astype(BF16), qg_row, kg_row, bd)


L_ROWS = 16
SKIP_LOG2_MARGIN = 160.0
SKIP_NORM_SLACK = 1.02


def _attn_kernel(slopes_ref, qT_ref, k_ref, vT_ref, lam_ref, g_ref, o_ref,
                 m_ref, acc_ref, bias_ref, ta_ref, tb_ref, mta_ref, mtb_ref, kn_ref, *, nk, tk, tq):
    h = pl.program_id(1)
    i = pl.program_id(2)
    sl = slopes_ref[h]
    lane_h = lax.broadcasted_iota(I32, (1, HEAD_W), 1)
    lane_v = lax.broadcasted_iota(I32, (1, LANES), 1)

    @pl.when(i == 0)
    def _():
        rel = (lax.broadcasted_iota(I32, (tk, tq), 1)
               - lax.broadcasted_iota(I32, (tk, tq), 0)).astype(F32)
        sr = sl * rel
        sa = sl * jnp.abs(rel)
        bias_ref[0] = jnp.concatenate([sr, sr], axis=1)
        bias_ref[1] = jnp.concatenate([sa, sa], axis=1)
        bias_ref[2] = jnp.concatenate([-sr, -sr], axis=1)

        def block_norm(j, carry):
            kf = k_ref[pl.ds(pl.multiple_of(j * tk, tk), tk), :].astype(F32)
            sq = kf * kf
            out = []
            for c in range(2):
                in_map = (lane_h >= c * QK_DIM) & (lane_h < (c + 1) * QK_DIM)
                n2 = jnp.sum(jnp.where(in_map, sq, 0.0), axis=1, keepdims=True)
                nmax = jnp.sqrt(jnp.max(n2, axis=0, keepdims=True))
                out.append(jnp.where(lane_v == j, nmax, carry[c]))
            return tuple(out)

        kn0, kn1 = lax.fori_loop(0, nk, block_norm,
                                 (jnp.zeros((1, LANES), F32), jnp.zeros((1, LANES), F32)))
        kn_ref[0:1, :] = kn0
        kn_ref[1:2, :] = kn1

    qT = qT_ref[0]
    zero_half = jnp.zeros((QK_DIM, tq), qT.dtype)
    q_cat = jnp.concatenate([jnp.concatenate([qT[:QK_DIM], zero_half], axis=0),
                             jnp.concatenate([zero_half, qT[QK_DIM:]], axis=0)], axis=1)
    ones = jnp.ones((L_ROWS, tk), BF16)
    m_ref[...] = jnp.full(m_ref.shape, -jnp.inf, F32)
    acc_ref[...] = jnp.zeros(acc_ref.shape, F32)

    def shift_of(j):
        gap = jnp.abs(jnp.full((1, 2 * tq), (i - j) * tq, I32)).astype(F32)
        return -sl * gap

    def scores(j, t_ref, mt_ref):
        kblk = k_ref[pl.ds(pl.multiple_of(j * tk, tk), tk), :]
        side = jnp.where(j < i, 0, jnp.where(j == i, 1, 2))
        t = _dot(kblk, q_cat) - bias_ref[side]
        t_ref[...] = t
        mt_ref[...] = jnp.max(t, axis=0, keepdims=True) + shift_of(j)

    def accumulate(j, t_ref, mt_ref):
        m_old = m_ref[...]
        m_new = jnp.maximum(m_old, mt_ref[...])
        p = jnp.exp2(t_ref[...] - (m_new - shift_of(j)))
        v_aug = jnp.concatenate([vT_ref[j], ones], axis=0)
        acc_ref[...] = jnp.exp2(m_old - m_new) * acc_ref[...] + _dot(v_aug, p.astype(BF16))
        m_ref[...] = m_new

    scores(i, ta_ref, mta_ref)

    qf = qT.astype(F32)
    qsq = qf * qf
    thr = None
    for c in range(2):
        rows_c = slice(c * QK_DIM, (c + 1) * QK_DIM)
        bq = jnp.sqrt(jnp.max(jnp.sum(qsq[rows_c], axis=0, keepdims=True), axis=1, keepdims=True))
        kn = kn_ref[c:c + 1, :]
        k_all = jnp.max(kn, axis=1, keepdims=True)
        k_diag = jnp.max(jnp.where(lane_v == i, kn, 0.0), axis=1, keepdims=True)
        thr_c = SKIP_NORM_SLACK * bq * (k_all + k_diag) + SKIP_LOG2_MARGIN
        thr = thr_c if thr is None else jnp.maximum(thr, thr_c)
    dist = lane_v.astype(F32)
    visit = (lane_v >= 1) & (sl * ((dist - 1.0) * tq + 1.0) <= thr)
    radius = jnp.sum(visit.astype(I32))
    lo = jnp.maximum(i - radius, 0)
    hi = jnp.minimum(i + radius, nk - 1)
    n_vis = hi - lo + 1

    def visited(n):
        j = lo + n - 1
        return jnp.where(n == 0, i, jnp.where(j < i, j, j + 1))

    def pair(jj, carry):
        n = 2 * jj
        scores(visited(n + 1), tb_ref, mtb_ref)
        accumulate(visited(n), ta_ref, mta_ref)
        scores(visited(n + 2), ta_ref, mta_ref)
        accumulate(visited(n + 1), tb_ref, mtb_ref)
        return carry

    lax.fori_loop(0, jnp.right_shift(n_vis - 1, 1), pair, 0)

    @pl.when(jnp.bitwise_and(n_vis, 1) == 1)
    def _():
        accumulate(visited(n_vis - 1), ta_ref, mta_ref)

    @pl.when(jnp.bitwise_and(n_vis, 1) == 0)
    def _():
        scores(visited(n_vis - 1), tb_ref, mtb_ref)
        accumulate(visited(n_vis - 2), ta_ref, mta_ref)
        accumulate(visited(n_vis - 1), tb_ref, mtb_ref)

    lp = lam_ref[...]
    lam = (jnp.exp(jnp.sum(lp[0:1] * lp[1:2], axis=-1, keepdims=True))
           - jnp.exp(jnp.sum(lp[2:3] * lp[3:4], axis=-1, keepdims=True)) + LAM_INIT)
    acc = acc_ref[...]
    a0 = acc[:, :tq]
    a1 = acc[:, tq:]
    o = (a0[:V_DIM] / a0[V_DIM:V_DIM + 1]
         - lam * (a1[:V_DIM] / a1[V_DIM:V_DIM + 1]))
    ms = jnp.mean(o * o, axis=0, keepdims=True)
    on = o * lax.rsqrt(ms + NORM_EPS) * g_ref[...] * (1.0 - LAM_INIT)
    o_ref[...] = on.T.astype(o_ref.dtype)


def _attention(qT, k, vT, diff_lambda, diff_norm_g, batch, seq, tq):
    nq = seq // tq
    assert nq <= LANES, "per-block key norms are kept one per lane"
    slopes = LOG2E * jnp.exp2(-8.0 * jnp.arange(1, HEADS + 1, dtype=F32) / HEADS)
    g_col = diff_norm_g.astype(F32).reshape(ATTN_W, 1)
    grid_spec = pltpu.PrefetchScalarGridSpec(
        num_scalar_prefetch=1,
        grid=(batch, HEADS, nq),
        in_specs=[
            pl.BlockSpec((1, HEAD_W, tq), lambda b, h, i, s: (b * nq + i, h, 0)),
            pl.BlockSpec((seq, HEAD_W), lambda b, h, i, s: (b, h)),
            pl.BlockSpec((nq, HEAD_W, tq), lambda b, h, i, s: (b, h, 0)),
            pl.BlockSpec((4, QK_DIM), lambda b, h, i, s: (0, 0)),
            pl.BlockSpec((HEAD_W, 1), lambda b, h, i, s: (h, 0)),
        ],
        out_specs=pl.BlockSpec((tq, HEAD_W), lambda b, h, i, s: (b * nq + i, h)),
        scratch_shapes=[pltpu.VMEM((1, 2 * tq), F32),
                        pltpu.VMEM((V_DIM + L_ROWS, 2 * tq), F32),
                        pltpu.VMEM((3, tq, 2 * tq), F32),
                        pltpu.VMEM((tq, 2 * tq), F32), pltpu.VMEM((tq, 2 * tq), F32),
                        pltpu.VMEM((1, 2 * tq), F32), pltpu.VMEM((1, 2 * tq), F32),
                        pltpu.VMEM((2, LANES), F32)],
    )
    return pl.pallas_call(
        functools.partial(_attn_kernel, nk=nq, tk=tq, tq=tq),
        grid_spec=grid_spec,
        out_shape=jax.ShapeDtypeStruct((batch * seq, ATTN_W), BF16),
        compiler_params=_cparams(("arbitrary", "arbitrary", "arbitrary")), name="diff_attn",
    )(slopes, qT, k, vT, diff_lambda.astype(F32), g_col)


SUBLANES = 8
REC_MAT_LEVELS = tuple(lv for lv in REC_LEVELS if 4 * lv < SUBLANES)


def _rec_boundary_row(lv, group, reverse):
    return group * 2 * lv + (lv if reverse else lv - 1)


def _rec_mats(reverse):
    c = REC_CHUNK
    t = np.arange(c)[:, None]
    u = np.arange(c)[None, :]
    mats = [(u >= t) if reverse else (u <= t)]
    for lv in REC_MAT_LEVELS:
        bd = _rec_boundary_row(lv, t // (2 * lv), reverse)
        mats.append((u >= bd) if reverse else (u <= bd))
    return jnp.asarray(np.concatenate(mats, axis=0).astype(np.float32), dtype=BF16)


def _rec_gates(q_ref, z_ref, v_ref, lbp_ref, mats_ref, n_chunks):
    c = REC_CHUNK
    lbp = lbp_ref[...]
    mx = jnp.maximum(lbp[0:1], lbp[1:2])
    e0 = jnp.exp(lbp[0:1] - mx)
    e1 = jnp.exp(lbp[1:2] - mx)
    lb = e0 / (e0 + e1)

    rows = [slice(n * c, (n + 1) * c) for n in range(n_chunks)]
    qs = [q_ref[sl, :] for sl in rows]
    vs = [v_ref[sl, :].astype(BF16) for sl in rows]
    gs, kks = [], []
    for sl in rows:
        z = z_ref[sl, :]
        sig = 1.0 / (1.0 + jnp.exp(-z))
        gs.append(jnp.log2(lb + (1.0 - lb) * sig))
        kks.append((1.0 - lb) * (1.0 / (1.0 + jnp.exp(z))))

    g_cat = jnp.concatenate(gs, axis=1)
    g1 = g_cat.astype(BF16)
    r1 = g_cat - g1.astype(F32)
    g2 = r1.astype(BF16)
    g3 = (r1 - g2.astype(F32)).astype(BF16)
    mats = mats_ref[...]
    stacked = (_dot(mats, g3) + _dot(mats, g2)) + _dot(mats, g1)
    return qs, kks, vs, stacked


def _rec_scores(qs, kks, vs, stacked, reverse):
    c = REC_CHUNK
    n_chunks = len(qs)
    row = lax.broadcasted_iota(I32, (c, HEAD_W), 0)
    ti = lax.broadcasted_iota(I32, (c, c), 0)
    si = lax.broadcasted_iota(I32, (c, c), 1)
    is_query, same = [], []
    for lv in REC_LEVELS:
        in_upper = (row & (2 * lv - 1)) >= lv
        is_query.append(jnp.logical_not(in_upper) if reverse else in_upper)
        shift = (2 * lv).bit_length() - 1
        same.append((ti >> shift) == (si >> shift))
    diag = ti == si

    o_intra, incs, qes, decays = [], [], [], []
    for n in range(n_chunks):
        lanes = slice(n * HEAD_W, (n + 1) * HEAD_W)
        b = stacked[0:c, lanes]
        q, kk = qs[n], kks[n]
        q_bf, kk_bf = q.astype(BF16), kk.astype(BF16)
        a = jnp.where(diag, _dot_nt(q_bf, kk_bf), 0.0)
        for m, lv in enumerate(REC_LEVELS):
            if lv in REC_MAT_LEVELS:
                mi = REC_MAT_LEVELS.index(lv)
                r = stacked[(mi + 1) * c:(mi + 2) * c, lanes]
            else:
                r = jnp.concatenate(
                    [jnp.broadcast_to(b[bd:bd + 1], (2 * lv, HEAD_W))
                     for bd in (_rec_boundary_row(lv, grp, reverse) for grp in range(c // (2 * lv)))],
                    axis=0)
            e = jnp.exp2(-jnp.abs(b - r))
            qt = jnp.where(is_query[m], q * e, 0.0).astype(BF16)
            kt = jnp.where(is_query[m], 0.0, kk * e).astype(BF16)
            part = _dot_nt(qt, kt)
            a = a + (part if 2 * lv == c else jnp.where(same[m], part, 0.0))
        b_end = b[0:1] if reverse else b[c - 1:c]
        o_intra.append(_dot(a.astype(BF16), vs[n]))
        incs.append(_dot_tn(vs[n], (kk * jnp.exp2(b_end - b)).astype(BF16)))
        qes.append((q * jnp.exp2(b)).astype(BF16))
        decays.append(jnp.exp2(b_end))
    return o_intra, incs, qes, decays


def _rec_scan(o_intra, incs, qes, decays, o_ref, st_ref, reverse):
    n_chunks = len(o_intra)
    st = st_ref[...]
    order = range(n_chunks - 1, -1, -1) if reverse else range(n_chunks)
    for n in order:
        o_ref[n * REC_CHUNK:(n + 1) * REC_CHUNK, :] = o_intra[n] + _dot_nt(qes[n], st.astype(BF16))
        st = st * decays[n] + incs[n]
    st_ref[...] = st


def _rec_kernel(qf_ref, zf_ref, vf_ref, qb_ref, zb_ref, vb_ref, lbpf_ref, lbpb_ref, matsf_ref,
                matsb_ref, of_ref, ob_ref, stf_ref, stb_ref, *, n_chunks):
    @pl.when(pl.program_id(2) == 0)
    def _():
        stf_ref[...] = jnp.zeros(stf_ref.shape, F32)
        stb_ref[...] = jnp.zeros(stb_ref.shape, F32)

    gates_f = _rec_gates(qf_ref, zf_ref, vf_ref, lbpf_ref, matsf_ref, n_chunks)
    gates_b = _rec_gates(qb_ref, zb_ref, vb_ref, lbpb_ref, matsb_ref, n_chunks)
    parts_f = _rec_scores(*gates_f, False)
    parts_b = _rec_scores(*gates_b, True)
    _rec_scan(*parts_f, of_ref, stf_ref, False)
    _rec_scan(*parts_b, ob_ref, stb_ref, True)


def _recurrence(rec, rec_lower_bound, batch, seq, tb):
    nb = seq // tb
    lbp = rec_lower_bound.astype(F32)

    def fwd(col):
        return lambda b, h, i: (b * nb + i, col + h)

    def bwd(col):
        return lambda b, h, i: (b * nb + nb - 1 - i, col + h)

    blk = (tb, HEAD_W)
    mat_shape = ((len(REC_MAT_LEVELS) + 1) * REC_CHUNK, REC_CHUNK)
    out = jax.ShapeDtypeStruct((batch * seq, REC_W), F32)
    return pl.pallas_call(
        functools.partial(_rec_kernel, n_chunks=tb // REC_CHUNK),
        grid=(batch, HEADS, nb),
        in_specs=[
            pl.BlockSpec(blk, fwd(0)), pl.BlockSpec(blk, fwd(HEADS)), pl.BlockSpec(blk, fwd(3 * HEADS)),
            pl.BlockSpec(blk, bwd(0)), pl.BlockSpec(blk, bwd(2 * HEADS)), pl.BlockSpec(blk, bwd(3 * HEADS)),
            pl.BlockSpec((2, HEAD_W), lambda b, h, i: (0, h)),
            pl.BlockSpec((2, HEAD_W), lambda b, h, i: (0, h)),
            pl.BlockSpec(mat_shape, lambda b, h, i: (0, 0)),
            pl.BlockSpec(mat_shape, lambda b, h, i: (0, 0)),
        ],
        out_specs=[pl.BlockSpec(blk, fwd(0)), pl.BlockSpec(blk, bwd(0))],
        out_shape=[out, out],
        scratch_shapes=[pltpu.VMEM((HEAD_W, HEAD_W), F32), pltpu.VMEM((HEAD_W, HEAD_W), F32)],
        compiler_params=_cparams(("parallel", "parallel", "arbitrary")), name="rec_bidir",
    )(rec, rec, rec, rec, rec, rec, lbp[0], lbp[1], _rec_mats(False), _rec_mats(True))


def _outproj_kernel(x_ref, ao_ref, of_ref, ob_ref, rg_ref, rng_ref, woa_ref, wor_ref,
                    fg_ref, wr_ref, x1_ref, x1acc_ref, aff_ref):
    parts = []
    for hd in range(HEADS):
        sl = slice(hd * HEAD_W, (hd + 1) * HEAD_W)
        o = of_ref[:, sl] + ob_ref[:, sl]
        parts.append((_rms(o, rng_ref[:, sl]) * rg_ref[:, sl]).astype(BF16))
    ro = jnp.concatenate(parts, axis=-1)
    x1 = x_ref[...] + _dot(ao_ref[...], woa_ref[...]) + _dot(ro, wor_ref[...])
    x1_ref[...] = x1
    x1acc_ref[...] = x1
    h2 = _rms(x1, fg_ref[...])
    logits = _dot(h2.astype(BF16), wr_ref[...])
    lane = lax.broadcasted_iota(I32, logits.shape, 1)
    logits = jnp.where(lane < N_EXPERTS, logits, -jnp.inf)
    mx = jnp.max(logits, axis=-1, keepdims=True)
    ex = jnp.exp(logits - mx)
    aff = ex / jnp.sum(ex, axis=-1, keepdims=True)
    aff_ref[...] = aff.T[:N_EXPERTS]


def _outproj(x2d, ao, o_f, o_b, rec, rec_norm_g, w_out, norm_ffn_g, w_router, tm):
    n_tok = x2d.shape[0]
    w_bf = w_out.astype(BF16)
    wr = jnp.pad(w_router.astype(F32), ((0, 0), (0, LANES - N_EXPERTS)))
    wr_bf = wr.astype(BF16)
    row = lambda i: (i, 0)
    fixed = lambda i: (0, 0)
    return pl.pallas_call(
        _outproj_kernel,
        grid=(n_tok // tm,),
        in_specs=[
            pl.BlockSpec((tm, D_MODEL), row),
            pl.BlockSpec((tm, ATTN_W), row),
            pl.BlockSpec((tm, REC_W), row),
            pl.BlockSpec((tm, REC_W), row),
            pl.BlockSpec((tm, REC_W), lambda i: (i, 4)),
            pl.BlockSpec((1, REC_W), fixed),
            pl.BlockSpec((ATTN_W, D_MODEL), fixed),
            pl.BlockSpec((REC_W, D_MODEL), fixed),
            pl.BlockSpec((1, D_MODEL), fixed),
            pl.BlockSpec((D_MODEL, LANES), fixed),
        ],
        out_specs=[pl.BlockSpec((tm, D_MODEL), row), pl.BlockSpec((tm, D_MODEL), row),
                   pl.BlockSpec((N_EXPERTS, tm), lambda i: (0, i))],
        out_shape=[jax.ShapeDtypeStruct((n_tok, D_MODEL), F32),
                   jax.ShapeDtypeStruct((n_tok, D_MODEL), F32),
                   jax.ShapeDtypeStruct((N_EXPERTS, n_tok), F32)],
        compiler_params=_cparams(("parallel",)), name="outproj_router",
    )(x2d, ao, o_f, o_b, rec, rec_norm_g.astype(F32).reshape(1, REC_W),
      w_bf[:ATTN_W], w_bf[ATTN_W:], norm_ffn_g.astype(F32).reshape(1, D_MODEL), wr_bf)


TOPK_BLK = 256


def _topk_kernel(aff_ref, tri_ref, idx_ref, gate_ref, *, n_tok, cap):
    aff = aff_ref[...]

    def search(step, cand):
        trial = cand | (jnp.int32(1) << (30 - step))
        n_ge = jnp.sum((aff >= pltpu.bitcast(trial, F32)).astype(I32), axis=1, keepdims=True)
        return jnp.where(n_ge >= cap, trial, cand)

    thr = pltpu.bitcast(lax.fori_loop(0, 31, search, jnp.zeros((N_EXPERTS, 1), I32)), F32)
    gt = aff > thr
    tie = aff == thr
    need = (cap - jnp.sum(gt.astype(I32), axis=1, keepdims=True)).astype(F32)

    def running_count(flags):
        f = flags.astype(F32).astype(BF16)
        carry = jnp.zeros((N_EXPERTS, 1), F32)
        out = []
        for j in range(n_tok // TOPK_BLK):
            pre = _dot(f[:, j * TOPK_BLK:(j + 1) * TOPK_BLK], tri_ref[...]) + carry
            out.append(pre)
            carry = pre[:, TOPK_BLK - 1:TOPK_BLK]
        return jnp.concatenate(out, axis=1)

    sel = jnp.logical_or(gt, jnp.logical_and(tie, running_count(tie) <= need))
    slot = running_count(sel).astype(I32) - 1

    lane = lax.broadcasted_iota(I32, (N_EXPERTS, n_tok), 1)
    packed = jnp.where(sel, (slot << 16) | lane, -1)
    val = aff
    for k in range((n_tok - 1).bit_length()):
        step = 1 << k
        dist = lane - (packed >> 16)
        moving = jnp.logical_and(packed >= 0, ((dist >> k) & 1) == 1)
        arriving = pltpu.roll(jnp.where(moving, packed, -1), n_tok - step, axis=1)
        arriving_val = pltpu.roll(val, n_tok - step, axis=1)
        lands = arriving >= 0
        packed = jnp.where(lands, arriving, jnp.where(moving, -1, packed))
        val = jnp.where(lands, arriving_val, val)
    idx_ref[...] = packed[:, :cap] & 0xFFFF
    gate_ref[...] = val[:, :cap]


def _topk(aff_t, cap):
    n_tok = aff_t.shape[1]
    assert n_tok <= 32768 and n_tok % TOPK_BLK == 0, "token index and slot share one int32"
    r = np.arange(TOPK_BLK)
    tri = jnp.asarray((r[:, None] <= r[None, :]).astype(np.float32), dtype=BF16)
    return pl.pallas_call(
        functools.partial(_topk_kernel, n_tok=n_tok, cap=cap),
        out_shape=[jax.ShapeDtypeStruct((N_EXPERTS, cap), I32),
                   jax.ShapeDtypeStruct((N_EXPERTS, cap), F32)],
        compiler_params=pltpu.CompilerParams(vmem_limit_bytes=VMEM_LIMIT), name="expert_topk",
    )(aff_t, tri)


ROW_DMA_UNROLL = 8


def _row_copy(hbm, vmem, t, p, sem, gather):
    if gather:
        return pltpu.make_async_copy(hbm.at[pl.ds(t, 1), :], vmem.at[pl.ds(p, 1), :], sem)
    return pltpu.make_async_copy(vmem.at[pl.ds(p, 1), :], hbm.at[pl.ds(t, 1), :], sem)


def _rows_start_loop(hbm, vmem, idx_ref, base, cap, sem, gather):
    def issue(blk, carry):
        for u in range(ROW_DMA_UNROLL):
            p = blk * ROW_DMA_UNROLL + u
            _row_copy(hbm, vmem, idx_ref[base + p], p, sem, gather).start()
        return carry
    lax.fori_loop(0, cap // ROW_DMA_UNROLL, issue, 0)


def _rows_start_inline(hbm, vmem, idx_ref, base, p0, count, sem, gather):
    for u in range(count):
        _row_copy(hbm, vmem, idx_ref[base + p0 + u], p0 + u, sem, gather).start()


def _rows_wait(hbm, vmem, cap, sem, gather):
    if gather:
        pltpu.make_async_copy(hbm.at[pl.ds(0, cap), :], vmem, sem).wait()
    else:
        pltpu.make_async_copy(vmem, hbm.at[pl.ds(0, cap), :], sem).wait()


def _moe_kernel(idx_ref, gate_ref, fg_ref, x_hbm, acc_in_hbm, wg_ref, wu_ref, wd_ref, out_hbm,
                xg, rows, xe, ye, sem_x, sem_r, sem_s, *, cap, nf):
    del acc_in_hbm
    e = pl.program_id(0)
    f = pl.program_id(1)
    last_e = pl.num_programs(0) - 1
    chunk = cap // nf

    @pl.when(f == 0)
    def _():
        @pl.when(e == 0)
        def _():
            _rows_start_loop(x_hbm, xg, idx_ref, 0, cap, sem_x, True)
            _rows_wait(x_hbm, xg, cap, sem_x, True)

        @pl.when(e > 0)
        def _():
            _rows_wait(out_hbm, rows, cap, sem_s, False)

        xe[...] = _rms(xg[...], fg_ref[...]).astype(BF16)
        ye[...] = jnp.zeros(ye.shape, F32)

    _rows_start_inline(x_hbm, xg, idx_ref, jnp.minimum(e + 1, last_e) * cap, f * chunk, chunk,
                       sem_x, True)
    _rows_start_inline(out_hbm, rows, idx_ref, e * cap, f * chunk, chunk, sem_r, True)

    xb = xe[...]
    hg = _dot(xb, wg_ref[0].astype(BF16))
    hu = _dot(xb, wu_ref[0].astype(BF16))
    hid = hg * (1.0 / (1.0 + jnp.exp(-hg))) * hu
    ye[...] += _dot(hid.astype(BF16), wd_ref[0].astype(BF16))

    @pl.when(f == nf - 1)
    def _():
        _rows_wait(x_hbm, xg, cap, sem_x, True)
        _rows_wait(out_hbm, rows, cap, sem_r, True)
        rows[...] = rows[...] + ye[...] * gate_ref[0]
        _rows_start_inline(out_hbm, rows, idx_ref, e * cap, 0, cap, sem_s, False)

        @pl.when(e == last_e)
        def _():
            _rows_wait(out_hbm, rows, cap, sem_s, False)


def _moe(x1, x1_acc, idx, gates, norm_ffn_g, w_gate, w_up, w_down, tf):
    n_tok = x1.shape[0]
    cap = idx.shape[1]
    nf = D_EXPERT // tf
    grid_spec = pltpu.PrefetchScalarGridSpec(
        num_scalar_prefetch=1,
        grid=(N_EXPERTS, nf),
        in_specs=[
            pl.BlockSpec((1, cap, 1), lambda e, f, s: (e, 0, 0)),
            pl.BlockSpec((1, D_MODEL), lambda e, f, s: (0, 0)),
            pl.BlockSpec(memory_space=pl.ANY),
            pl.BlockSpec(memory_space=pl.ANY),
            pl.BlockSpec((1, D_MODEL, tf), lambda e, f, s: (e, 0, f)),
            pl.BlockSpec((1, D_MODEL, tf), lambda e, f, s: (e, 0, f)),
            pl.BlockSpec((1, tf, D_MODEL), lambda e, f, s: (e, f, 0)),
        ],
        out_specs=pl.BlockSpec(memory_space=pl.ANY),
        scratch_shapes=[pltpu.VMEM((cap, D_MODEL), F32), pltpu.VMEM((cap, D_MODEL), F32),
                        pltpu.VMEM((cap, D_MODEL), BF16), pltpu.VMEM((cap, D_MODEL), F32),
                        pltpu.SemaphoreType.DMA(()), pltpu.SemaphoreType.DMA(()),
                        pltpu.SemaphoreType.DMA(())],
    )
    return pl.pallas_call(
        functools.partial(_moe_kernel, cap=cap, nf=nf),
        grid_spec=grid_spec,
        out_shape=jax.ShapeDtypeStruct((n_tok, D_MODEL), F32),
        input_output_aliases={4: 0},
        compiler_params=_cparams(("arbitrary", "arbitrary")), name="expert_ffn",
    )(idx.reshape(N_EXPERTS * cap), gates.reshape(N_EXPERTS, cap, 1),
      norm_ffn_g.astype(F32).reshape(1, D_MODEL),
      x1, x1_acc, w_gate, w_up, w_down)


def _ple_kernel(x_ref, p_ref, g_ref, wg_ref, wp_ref, o_ref):
    x = x_ref[...]
    zg = _dot(_rms(x, g_ref[...]).astype(BF16), wg_ref[...])
    gate = 1.0 / (1.0 + jnp.exp(-zg))
    o_ref[...] = x + gate * _dot(p_ref[...].astype(BF16), wp_ref[...])


def _ple(x2, p2d, norm_ple_g, w_ple_gate, w_ple_proj, tm):
    n_tok = x2.shape[0]
    row = lambda i: (i, 0)
    fixed = lambda i: (0, 0)
    return pl.pallas_call(
        _ple_kernel,
        grid=(n_tok // tm,),
        in_specs=[pl.BlockSpec((tm, D_MODEL), row), pl.BlockSpec((tm, PLE_DIM), row),
                  pl.BlockSpec((1, D_MODEL), fixed), pl.BlockSpec((D_MODEL, D_MODEL), fixed),
                  pl.BlockSpec((PLE_DIM, D_MODEL), fixed)],
        out_specs=pl.BlockSpec((tm, D_MODEL), row),
        out_shape=jax.ShapeDtypeStruct((n_tok, D_MODEL), F32),
        compiler_params=_cparams(("parallel",)), name="ple_gate",
    )(x2, p2d, norm_ple_g.astype(F32).reshape(1, D_MODEL), w_ple_gate.astype(BF16),
      w_ple_proj.astype(BF16))


def _tiles(batch, seq):
    t_attn = min(512, seq)
    n_blk = batch * seq // t_attn
    t_proj = t_attn * (2 if n_blk % 2 == 0 else 1)
    t_rec = min(1024, seq)
    t_row = min(256, seq)
    return t_attn, t_proj, t_rec, t_row


def _layer(x, p, norm_mix_g, w_in, q_norm_g, k_norm_g, diff_lambda, diff_norm_g, rec_lower_bound,
           rec_norm_g, w_out, norm_ffn_g, w_router, w_expert_gate, w_expert_up, w_expert_down,
           norm_ple_g, w_ple_gate, w_ple_proj):
    batch, seq, _ = x.shape
    n_tok = batch * seq
    t_attn, t_proj, t_rec, t_row = _tiles(batch, seq)
    x2d = x.reshape(n_tok, D_MODEL)
    qT, k, vT, rec = _projections(x2d, norm_mix_g[0], w_in[0], q_norm_g[0], k_norm_g[0], t_proj,
                                  t_attn)
    ao = _attention(qT, k, vT, diff_lambda[0], diff_norm_g[0], batch, seq, t_attn)
    o_f, o_b = _recurrence(rec, rec_lower_bound, batch, seq, t_rec)
    x1, x1_acc, aff = _outproj(x2d, ao, o_f, o_b, rec, rec_norm_g[0], w_out[0], norm_ffn_g[0],
                               w_router[0], t_row)
    cap = max(1, (CAPACITY_FACTOR * n_tok) // N_EXPERTS)
    idx, gates = _topk(aff, cap)
    x2 = _moe(x1, x1_acc, idx, gates, norm_ffn_g[0], w_expert_gate[0], w_expert_up[0],
              w_expert_down[0], tf=256)
    y = _ple(x2, p[0].reshape(n_tok, PLE_DIM), norm_ple_g[0], w_ple_gate[0], w_ple_proj[0], t_row)
    return y.reshape(x.shape)


def kernel(x_prompt, x_sample, p_prompt, p_sample, norm_mix_g, w_in, q_norm_g, k_norm_g, diff_lambda, diff_norm_g, rec_lower_bound, rec_norm_g, w_out, norm_ffn_g, w_router, w_expert_gate, w_expert_up, w_expert_down, norm_ple_g, w_ple_gate, w_ple_proj):
    weights = (norm_mix_g, w_in, q_norm_g, k_norm_g, diff_lambda, diff_norm_g, rec_lower_bound,
               rec_norm_g, w_out, norm_ffn_g, w_router, w_expert_gate, w_expert_up, w_expert_down,
               norm_ple_g, w_ple_gate, w_ple_proj)
    return (_layer(x_prompt, p_prompt, *weights), _layer(x_sample, p_sample, *weights))
```
